```python
import jax
import jax.numpy as jnp
from jax import lax
import numpy as np


D_MODEL = 1024
BATCH = 4
SEQ = 4096
DEPTH = 2

N_A_LAYERS = DEPTH // 2
N_B_LAYERS = DEPTH - N_A_LAYERS
HEAD_DIM = 64
MIX_WIDTH = D_MODEL
MEM_HEADS = 4
MEM_WIDTH = MEM_HEADS * HEAD_DIM
MAIN_WIDTH = MIX_WIDTH - MEM_WIDTH
CHUNK = 128
A_GROUPS = 6
A_GROUP_DIM = MAIN_WIDTH // A_GROUPS
FOX_HEADS = MAIN_WIDTH // HEAD_DIM
Q_BLOCK = 128
N_MEM = 256
D_FF = -(-8 * D_MODEL // (3 * 256)) * 256
RMS_EPS = 1e-6
LN_EPS = 1e-5
FORGET_BIAS = 4.0

kernel_name = "yoco_gmlp_fox_memory_hybrid"


def rmsnorm(x, g):
    xf = x.astype(jnp.float32)
    y = xf * lax.rsqrt(jnp.mean(xf * xf, axis=-1, keepdims=True) + RMS_EPS)
    return (y * g.astype(jnp.float32)).astype(x.dtype)


def layernorm(x, g, b):
    xf = x.astype(jnp.float32)
    mu = jnp.mean(xf, axis=-1, keepdims=True)
    xc = xf - mu
    y = xc * lax.rsqrt(jnp.mean(xc * xc, axis=-1, keepdims=True) + LN_EPS)
    return (y * g.astype(jnp.float32) + b.astype(jnp.float32)).astype(x.dtype)


def memory_attention(q_mem, mem_n, w_mem_kv):
    B, S, _ = q_mem.shape
    M = mem_n.shape[1]
    k, v = jnp.split(mem_n @ w_mem_kv, 2, axis=-1)
    q = q_mem.reshape(B, S, MEM_HEADS, HEAD_DIM)
    k = k.reshape(B, M, MEM_HEADS, HEAD_DIM)
    v = v.reshape(B, M, MEM_HEADS, HEAD_DIM)
    logits = jnp.einsum('bshd,bmhd->bhsm', q, k).astype(jnp.float32) * (HEAD_DIM ** -0.5)
    p = jax.nn.softmax(logits, axis=-1).astype(v.dtype)
    o = jnp.einsum('bhsm,bmhd->bshd', p, v)
    return o.reshape(B, S, MEM_WIDTH)


def chunked_spatial_gating(u, v, w_s, b_s, ln_g, ln_b):
    B, S, _ = u.shape
    u = jax.nn.gelu(u)
    v = layernorm(jax.nn.gelu(v), ln_g, ln_b)
    nc = S // CHUNK
    vc = v.reshape(B, nc, CHUNK, A_GROUPS, A_GROUP_DIM)
    mask = jnp.tril(jnp.ones((CHUNK, CHUNK), dtype=bool))
    w = jnp.where(mask[None], w_s, jnp.zeros((), w_s.dtype))
    s = jnp.einsum('gts,bcsgd->bctgd', w, vc) + b_s.T[None, None, :, :, None]
    return u * s.reshape(B, S, MAIN_WIDTH)


def forgetting_attention(q, k, v, log_f):
    B, S, H, Dh = q.shape
    c = jnp.cumsum(log_f, axis=1)
    cT = c.transpose(0, 2, 1)
    nb = S // Q_BLOCK
    qb = q.reshape(B, nb, Q_BLOCK, H, Dh).transpose(1, 0, 2, 3, 4)
    cb = cT.reshape(B, H, nb, Q_BLOCK).transpose(2, 0, 1, 3)
    kpos = jnp.arange(S)
    scale = Dh ** -0.5

    def block(args):
        i, qi, ci = args
        qpos = i * Q_BLOCK + jnp.arange(Q_BLOCK)
        logits = jnp.einsum('bthd,bshd->bhts', qi, k).astype(jnp.float32) * scale
        logits = logits + ci[..., :, None] - cT[..., None, :]
        mask = kpos[None, :] <= qpos[:, None]
        logits = jnp.where(mask, logits, -jnp.inf)
        p = jax.nn.softmax(logits, axis=-1).astype(v.dtype)
        return jnp.einsum('bhts,bshd->bthd', p, v)

    o = lax.map(block, (jnp.arange(nb), qb, cb))
    return o.transpose(1, 0, 2, 3, 4).reshape(B, S, H * Dh)


def setup_inputs(seed: int = 0) -> dict:
    key = jax.random.key(seed)
    ks = jax.random.split(key, 24)
    f32 = jnp.float32

    def dense(k, shape, fan_in):
        return jax.random.normal(k, shape, f32) * (fan_in ** -0.5)

    def gain(k, shape):
        return 1.0 + 0.05 * jax.random.normal(k, shape, f32)

    x = jax.random.normal(ks[0], (BATCH, SEQ, D_MODEL), f32)
    mem = jax.random.normal(ks[1], (BATCH, N_MEM, D_MODEL), f32)
    w_shared_kv = jnp.concatenate([
        dense(ks[19], (D_MODEL, 2 * MAIN_WIDTH), D_MODEL),
        0.1 * dense(ks[20], (D_MODEL, FOX_HEADS), D_MODEL)], axis=-1)
    return {
        'x': x,
        'mem': mem,
        'ln_mix_pre': gain(ks[2], (DEPTH, D_MODEL)),
        'ln_mix_post': gain(ks[3], (DEPTH, D_MODEL)),
        'ln_ffn_pre': gain(ks[4], (DEPTH, D_MODEL)),
        'ln_ffn_post': gain(ks[5], (DEPTH, D_MODEL)),
        'ln_mem': gain(ks[6], (DEPTH, D_MODEL)),
        'w_mem_kv': dense(ks[7], (DEPTH, D_MODEL, 2 * MEM_WIDTH), D_MODEL),
        'w_out': dense(ks[8], (DEPTH, MIX_WIDTH, D_MODEL), MIX_WIDTH),
        'w_ffn_gate': dense(ks[9], (DEPTH, D_MODEL, D_FF), D_MODEL),
        'w_ffn_up': dense(ks[10], (DEPTH, D_MODEL, D_FF), D_MODEL),
        'w_ffn_down': dense(ks[11], (DEPTH, D_FF, D_MODEL), D_FF),
        'w_in_a': dense(ks[12], (N_A_LAYERS, D_MODEL, 2 * MAIN_WIDTH + MEM_WIDTH), D_MODEL),
        'w_spatial': dense(ks[13], (N_A_LAYERS, A_GROUPS, CHUNK, CHUNK), CHUNK),
        'b_spatial': 1.0 + 0.05 * jax.random.normal(ks[14], (N_A_LAYERS, A_GROUPS, CHUNK), f32),
        'ln_v_g': gain(ks[15], (N_A_LAYERS, MAIN_WIDTH)),
        'ln_v_b': 0.02 * jax.random.normal(ks[16], (N_A_LAYERS, MAIN_WIDTH), f32),
        'ln_shared': gain(ks[17], (D_MODEL,)),
        'w_shared_kv': w_shared_kv,
        'b_forget': FORGET_BIAS + 0.1 * jax.random.normal(ks[18], (FOX_HEADS,), f32),
        'w_in_b': dense(ks[21], (N_B_LAYERS, D_MODEL, MAIN_WIDTH + MEM_WIDTH), D_MODEL),
    }


def reference(x, mem, ln_mix_pre, ln_mix_post, ln_ffn_pre, ln_ffn_post, ln_mem,
              w_mem_kv, w_out, w_ffn_gate, w_ffn_up, w_ffn_down, w_in_a,
              w_spatial, b_spatial, ln_v_g, ln_v_b, ln_shared, w_shared_kv,
              b_forget, w_in_b):
    B, S, _ = x.shape
    h = x
    k_s = v_s = log_f_s = None
    for layer in range(DEPTH):
        a = rmsnorm(h, ln_mix_pre[layer])
        mem_n = rmsnorm(mem, ln_mem[layer])
        if layer < N_A_LAYERS:
            proj = a @ w_in_a[layer]
            u = proj[..., :MAIN_WIDTH]
            v = proj[..., MAIN_WIDTH:2 * MAIN_WIDTH]
            q_mem = proj[..., 2 * MAIN_WIDTH:]
            main = chunked_spatial_gating(u, v, w_spatial[layer], b_spatial[layer],
                                          ln_v_g[layer], ln_v_b[layer])
        else:
            if layer == N_A_LAYERS:
                s_in = rmsnorm(h, ln_shared)
                kvf = s_in @ w_shared_kv
                k_s = kvf[..., :MAIN_WIDTH].reshape(B, S, FOX_HEADS, HEAD_DIM)
                v_s = kvf[..., MAIN_WIDTH:2 * MAIN_WIDTH].reshape(B, S, FOX_HEADS, HEAD_DIM)
                log_f_s = jax.nn.log_sigmoid(kvf[..., 2 * MAIN_WIDTH:].astype(jnp.float32)
                                             + b_forget.astype(jnp.float32))
            proj = a @ w_in_b[layer - N_A_LAYERS]
            q = proj[..., :MAIN_WIDTH].reshape(B, S, FOX_HEADS, HEAD_DIM)
            q_mem = proj[..., MAIN_WIDTH:]
            main = forgetting_attention(q, k_s, v_s, log_f_s)
        mixed = jnp.concatenate([main, memory_attention(q_mem, mem_n, w_mem_kv[layer])], axis=-1)
        h = h + rmsnorm(mixed @ w_out[layer], ln_mix_post[layer])
        f = rmsnorm(h, ln_ffn_pre[layer])
        f = (jax.nn.silu(f @ w_ffn_gate[layer]) * (f @ w_ffn_up[layer])) @ w_ffn_down[layer]
        h = h + rmsnorm(f, ln_ffn_post[layer])
    return h
```

```python
import functools

import jax
import jax.numpy as jnp
from jax import lax
from jax.experimental import pallas as pl
from jax.experimental.pallas import tpu as pltpu

F32 = jnp.float32
BF16 = jnp.bfloat16

D_MODEL = 1024
HEAD_DIM = 64
MEM_HEADS = 4
MEM_WIDTH = MEM_HEADS * HEAD_DIM
MAIN_WIDTH = D_MODEL - MEM_WIDTH
CHUNK = 128
A_GROUPS = 6
FOX_HEADS = MAIN_WIDTH // HEAD_DIM
RMS_EPS = 1e-6
LN_EPS = 1e-5
SCALE = HEAD_DIM ** -0.5
LANES = 128
VMEM_LIMIT = 56 * 1024 * 1024

NT_DIMS = (((1,), (1,)), ((), ()))


def _dot(a, b):
    return jnp.dot(a, b, preferred_element_type=F32)


def _dot_nt(a, b):
    return lax.dot_general(a, b, NT_DIMS, preferred_element_type=F32)


def _rms(x, g):
    return (x * lax.rsqrt(jnp.mean(x * x, axis=-1, keepdims=True) + RMS_EPS)) * g


def _params(*sem):
    return pltpu.CompilerParams(dimension_semantics=sem, vmem_limit_bytes=VMEM_LIMIT)


def _full(shape):
    n = len(shape)
    return pl.BlockSpec(shape, lambda *_: (0,) * n)


def _memkv_kernel(mem_ref, g_ref, w_ref, o_ref):
    n = _rms(mem_ref[0], g_ref[0])
    o_ref[0, 0] = _dot(n.astype(BF16), w_ref[0]).astype(BF16)


def _memkv(mem, ln_mem, w_mem_kv):
    depth = ln_mem.shape[0]
    b, m, d = mem.shape
    n_out = w_mem_kv.shape[-1]
    return pl.pallas_call(
        _memkv_kernel,
        grid=(depth, b),
        in_specs=[
            pl.BlockSpec((1, m, d), lambda l, i: (i, 0, 0)),
            pl.BlockSpec((1, 1, d), lambda l, i: (l, 0, 0)),
            pl.BlockSpec((1, d, n_out), lambda l, i: (l, 0, 0)),
        ],
        out_specs=pl.BlockSpec((1, 1, m, n_out), lambda l, i: (l, i, 0, 0)),
        out_shape=jax.ShapeDtypeStruct((depth, b, m, n_out), BF16),
        compiler_params=_params("arbitrary", "arbitrary"),
        name="memkv",
    )(mem, ln_mem.reshape(depth, 1, d), w_mem_kv)


def _memory_attention(qm, kv):
    k = kv[:, :MEM_WIDTH]
    v = kv[:, MEM_WIDTH:]
    head_of_lane = lax.broadcasted_iota(jnp.int32, (1, MEM_WIDTH), 1) // HEAD_DIM
    out = jnp.zeros(qm.shape, F32)
    for hh in range(MEM_HEADS):
        sel = head_of_lane == hh
        s = _dot_nt(jnp.where(sel, qm, jnp.zeros_like(qm)), k)
        p = jnp.exp(s - jnp.max(s, axis=-1, keepdims=True))
        l = jnp.sum(p, axis=-1, keepdims=True)
        o = _dot(p.astype(BF16), v)
        out = jnp.where(sel, o / l, out)
    return out


def _mixer_a_kernel(h_ref, kv_ref, g_pre_ref, w_in_ref, wsp_ref, bsp_ref, lng_ref, lnb_ref,
                    w_out_ref, g_post_ref, o_ref, mixed_ref):
    tm = h_ref.shape[0]
    h = h_ref[...]
    a = _rms(h, g_pre_ref[...]).astype(BF16)
    proj = _dot(a, w_in_ref[...])
    u = jax.nn.gelu(proj[:, :MAIN_WIDTH])
    v = jax.nn.gelu(proj[:, MAIN_WIDTH:2 * MAIN_WIDTH])
    mu = jnp.mean(v, axis=-1, keepdims=True)
    vc = v - mu
    vn = vc * lax.rsqrt(jnp.mean(vc * vc, axis=-1, keepdims=True) + LN_EPS)
    vn = (vn * lng_ref[...] + lnb_ref[...]).astype(BF16)
    row = lax.broadcasted_iota(jnp.int32, (CHUNK, CHUNK), 0)
    col = lax.broadcasted_iota(jnp.int32, (CHUNK, CHUNK), 1)
    causal = row >= col
    for g in range(A_GROUPS):
        w = jnp.where(causal, wsp_ref[g], 0.0).astype(BF16)
        bias = bsp_ref[g]
        cols = slice(g * CHUNK, (g + 1) * CHUNK)
        for c in range(tm // CHUNK):
            rows = slice(c * CHUNK, (c + 1) * CHUNK)
            s = _dot(w, vn[rows, cols]) + bias
            mixed_ref[rows, cols] = (u[rows, cols] * s).astype(BF16)
    qm = (proj[:, 2 * MAIN_WIDTH:] * SCALE).astype(BF16)
    mixed_ref[:, MAIN_WIDTH:] = _memory_attention(qm, kv_ref[0]).astype(BF16)
    y = _dot(mixed_ref[...], w_out_ref[...])
    o_ref[...] = h + _rms(y, g_post_ref[...])


def _mixer_a(h, kv, g_pre, w_in, wsp, bsp, lng, lnb, w_out, g_post, seq, tm):
    t, d = h.shape
    per_batch = seq // tm
    return pl.pallas_call(
        _mixer_a_kernel,
        grid=(t // tm,),
        in_specs=[
            pl.BlockSpec((tm, d), lambda i: (i, 0)),
            pl.BlockSpec((1,) + kv.shape[1:], lambda i: (i // per_batch, 0, 0)),
            _full(g_pre.shape), _full(w_in.shape), _full(wsp.shape), _full(bsp.shape),
            _full(lng.shape), _full(lnb.shape), _full(w_out.shape), _full(g_post.shape),
        ],
        out_specs=pl.BlockSpec((tm, d), lambda i: (i, 0)),
        out_shape=jax.ShapeDtypeStruct((t, d), F32),
        scratch_shapes=[pltpu.VMEM((tm, d), BF16)],
        compiler_params=_params("arbitrary"),
        name="mixer_a",
    )(h, kv, g_pre, w_in, wsp, bsp, lng, lnb, w_out, g_post)


def _ffn_kernel(h_ref, g_pre_ref, wg_ref, wu_ref, wd_ref, g_post_ref, o_ref):
    h = h_ref[...]
    f = _rms(h, g_pre_ref[...]).astype(BF16)
    gate = _dot(f, wg_ref[...])
    up = _dot(f, wu_ref[...])
    act = (jax.nn.silu(gate) * up).astype(BF16)
    y = _dot(act, wd_ref[...])
    o_ref[...] = h + _rms(y, g_post_ref[...])


def _ffn(h, g_pre, wg, wu, wd, g_post, tm):
    t, d = h.shape
    return pl.pallas_call(
        _ffn_kernel,
        grid=(t // tm,),
        in_specs=[
            pl.BlockSpec((tm, d), lambda i: (i, 0)),
            _full(g_pre.shape), _full(wg.shape), _full(wu.shape), _full(wd.shape),
            _full(g_post.shape),
        ],
        out_specs=pl.BlockSpec((tm, d), lambda i: (i, 0)),
        out_shape=jax.ShapeDtypeStruct((t, d), F32),
        compiler_params=_params("arbitrary"),
        name="ffn",
    )(h, g_pre, wg, wu, wd, g_post)


def _proj_b_kernel(h_ref, g_sh_ref, g_pre_ref, w_kv_ref, w_f_ref, b_f_ref, w_in_ref,
                   q_ref, k_ref, v_ref, qm_ref, c_ref, carry_ref):
    tm = h_ref.shape[1]

    @pl.when(pl.program_id(1) == 0)
    def _():
        carry_ref[...] = jnp.zeros_like(carry_ref)

    h = h_ref[0]
    hn = h * lax.rsqrt(jnp.mean(h * h, axis=-1, keepdims=True) + RMS_EPS)
    s_in = (hn * g_sh_ref[...]).astype(BF16)
    a = (hn * g_pre_ref[...]).astype(BF16)

    kv = _dot(s_in, w_kv_ref[...])
    k_ref[0] = kv[:, :MAIN_WIDTH].astype(BF16)
    v_ref[0] = kv[:, MAIN_WIDTH:].astype(BF16)

    log_f = jax.nn.log_sigmoid(_dot(s_in, w_f_ref[...]) + b_f_ref[...])
    p1 = log_f.astype(BF16)
    r1 = log_f - p1.astype(F32)
    p2 = r1.astype(BF16)
    p3 = (r1 - p2.astype(F32)).astype(BF16)
    row = lax.broadcasted_iota(jnp.int32, (tm, tm), 0)
    col = lax.broadcasted_iota(jnp.int32, (tm, tm), 1)
    tri = (row >= col).astype(BF16)
    c = _dot(tri, p1) + _dot(tri, p2) + _dot(tri, p3) + carry_ref[...]
    c_ref[0] = c
    carry_ref[...] = c[tm - 1:tm, :]

    pr = _dot(a, w_in_ref[...])
    q_ref[0] = (pr[:, :MAIN_WIDTH] * SCALE).astype(BF16)
    qm_ref[0] = (pr[:, MAIN_WIDTH:] * SCALE).astype(BF16)


def _proj_b(h3, g_sh, g_pre, w_kv, w_f, b_f, w_in, tm):
    b, s, d = h3.shape
    tile = lambda w: pl.BlockSpec((1, tm, w), lambda i, j: (i, j, 0))
    return pl.pallas_call(
        _proj_b_kernel,
        grid=(b, s // tm),
        in_specs=[tile(d), _full(g_sh.shape), _full(g_pre.shape), _full(w_kv.shape),
                  _full(w_f.shape), _full(b_f.shape), _full(w_in.shape)],
        out_specs=[tile(MAIN_WIDTH), tile(MAIN_WIDTH), tile(MAIN_WIDTH), tile(MEM_WIDTH),
                   tile(LANES)],
        out_shape=[
            jax.ShapeDtypeStruct((b, s, MAIN_WIDTH), BF16),
            jax.ShapeDtypeStruct((b, s, MAIN_WIDTH), BF16),
            jax.ShapeDtypeStruct((b, s, MAIN_WIDTH), BF16),
            jax.ShapeDtypeStruct((b, s, MEM_WIDTH), BF16),
            jax.ShapeDtypeStruct((b, s, LANES), F32),
        ],
        scratch_shapes=[pltpu.VMEM((1, LANES), F32)],
        compiler_params=_params("arbitrary", "arbitrary"),
        name="proj_b",
    )(h3, g_sh, g_pre, w_kv, w_f, b_f, w_in)


def _fox_kernel(q_ref, k_ref, v_ref, c_tok_ref, c_row_ref, o_ref, m_ref, l_ref, acc_ref):
    tq = q_ref.shape[1]
    hp = pl.program_id(1)
    qi = pl.program_id(2)
    q = q_ref[0]
    lane = lax.broadcasted_iota(jnp.int32, (1, LANES), 1)
    row = lax.broadcasted_iota(jnp.int32, (tq, tq), 0)
    col = lax.broadcasted_iota(jnp.int32, (tq, tq), 1)
    causal = row >= col
    outs = []
    for hh in range(2):
        head = 2 * hp + hh
        qh = jnp.where(lane // HEAD_DIM == hh, q, jnp.zeros_like(q))
        c_t = jnp.sum(jnp.where(lane == head, c_tok_ref[0], 0.0), axis=-1, keepdims=True)
        m_ref[...] = jnp.full(m_ref.shape, -jnp.inf, F32)
        l_ref[...] = jnp.zeros(l_ref.shape, F32)
        acc_ref[...] = jnp.zeros(acc_ref.shape, F32)

        def block(j, masked):
            start = pl.multiple_of(j * tq, tq)
            ks = k_ref[0, pl.ds(start, tq), :]
            vs = v_ref[0, pl.ds(start, tq), :]
            c_s = c_row_ref[0, pl.ds(head, 1), pl.ds(start, tq)]
            s = _dot_nt(qh, ks) + c_t - c_s
            if masked:
                s = jnp.where(causal, s, -jnp.inf)
            m_old = m_ref[...]
            m_new = jnp.maximum(m_old, jnp.max(s, axis=-1, keepdims=True))
            alpha = jnp.exp(m_old - m_new)
            p = jnp.exp(s - m_new)
            l_ref[...] = alpha * l_ref[...] + jnp.sum(p, axis=-1, keepdims=True)
            acc_ref[...] = alpha * acc_ref[...] + _dot(p.astype(BF16), vs)
            m_ref[...] = m_new

        def body(j, carry):
            block(j, False)
            return carry

        lax.fori_loop(0, qi, body, 0)
        block(qi, True)
        outs.append(acc_ref[...] / l_ref[...])
    o_ref[0] = jnp.where(lane // HEAD_DIM == 0, outs[0], outs[1]).astype(BF16)


def _fox_attention(q, k, v, c_tok, c_row, tq):
    b, s, _ = q.shape
    pairs = FOX_HEADS // 2
    return pl.pallas_call(
        _fox_kernel,
        grid=(b, pairs, s // tq),
        in_specs=[
            pl.BlockSpec((1, tq, LANES), lambda i, p, j: (i, j, p)),
            pl.BlockSpec((1, s, LANES), lambda i, p, j: (i, 0, p)),
            pl.BlockSpec((1, s, LANES), lambda i, p, j: (i, 0, p)),
            pl.BlockSpec((1, tq, LANES), lambda i, p, j: (i, j, 0)),
            pl.BlockSpec((1,) + c_row.shape[1:], lambda i, p, j: (i, 0, 0)),
        ],
        out_specs=pl.BlockSpec((1, tq, LANES), lambda i, p, j: (i, j, p)),
        out_shape=jax.ShapeDtypeStruct((b, s, MAIN_WIDTH), BF16),
        scratch_shapes=[pltpu.VMEM((tq, 1), F32), pltpu.VMEM((tq, 1), F32),
                        pltpu.VMEM((tq, LANES), F32)],
        compiler_params=_params("arbitrary", "arbitrary", "arbitrary"),
        name="fox_attention",
    )(q, k, v, c_tok, c_row)


def _mixer_b_kernel(h_ref, attn_ref, qm_ref, kv_ref, w_out_ref, g_post_ref, o_ref):
    h = h_ref[...]
    mem_o = _memory_attention(qm_ref[...], kv_ref[0]).astype(BF16)
    y = _dot(attn_ref[...], w_out_ref[:MAIN_WIDTH, :]) + _dot(mem_o, w_out_ref[MAIN_WIDTH:, :])
    o_ref[...] = h + _rms(y, g_post_ref[...])


def _mixer_b(h, attn, qm, kv, w_out, g_post, seq, tm):
    t, d = h.shape
    per_batch = seq // tm
    tile = lambda w: pl.BlockSpec((tm, w), lambda i: (i, 0))
    return pl.pallas_call(
        _mixer_b_kernel,
        grid=(t // tm,),
        in_specs=[tile(d), tile(MAIN_WIDTH), tile(MEM_WIDTH),
                  pl.BlockSpec((1,) + kv.shape[1:], lambda i: (i // per_batch, 0, 0)),
                  _full(w_out.shape), _full(g_post.shape)],
        out_specs=tile(d),
        out_shape=jax.ShapeDtypeStruct((t, d), F32),
        compiler_params=_params("arbitrary"),
        name="mixer_b",
    )(h, attn, qm, kv, w_out, g_post)


def kernel(x, mem, ln_mix_pre, ln_mix_post, ln_ffn_pre, ln_ffn_post, ln_mem, w_mem_kv, w_out,
           w_ffn_gate, w_ffn_up, w_ffn_down, w_in_a, w_spatial, b_spatial, ln_v_g, ln_v_b,
           ln_shared, w_shared_kv, b_forget, w_in_b):
    b, s, d = x.shape
    t = b * s
    row = lambda g: g.reshape(1, -1)
    bf = lambda w: w.astype(BF16)

    kv_mem = _memkv(mem, ln_mem, bf(w_mem_kv))
    h = x.reshape(t, d)

    h = _mixer_a(h, kv_mem[0], row(ln_mix_pre[0]), bf(w_in_a[0]), w_spatial[0],
                 b_spatial[0].reshape(A_GROUPS, CHUNK, 1), row(ln_v_g[0]), row(ln_v_b[0]),
                 bf(w_out[0]), row(ln_mix_post[0]), s, 256)
    h = _ffn(h, row(ln_ffn_pre[0]), bf(w_ffn_gate[0]), bf(w_ffn_up[0]), bf(w_ffn_down[0]),
             row(ln_ffn_post[0]), 256)

    pad = LANES - FOX_HEADS
    w_f = jnp.pad(w_shared_kv[:, 2 * MAIN_WIDTH:], ((0, 0), (0, pad)))
    b_f = jnp.pad(b_forget, (0, pad)).reshape(1, LANES)
    q, k, v, qm, c_tok = _proj_b(h.reshape(b, s, d), row(ln_shared), row(ln_mix_pre[1]),
                                 bf(w_shared_kv[:, :2 * MAIN_WIDTH]), bf(w_f), b_f,
                                 bf(w_in_b[0]), 512)
    c_row = jnp.swapaxes(c_tok[:, :, :16], 1, 2)
    attn = _fox_attention(q, k, v, c_tok, c_row, 256)
    h = _mixer_b(h, attn.reshape(t, MAIN_WIDTH), qm.reshape(t, MEM_WIDTH), kv_mem[1],
                 bf(w_out[1]), row(ln_mix_post[1]), s, 256)
    h = _ffn(h, row(ln_ffn_pre[1]), bf(w_ffn_gate[1]), bf(w_ffn_up[1]), bf(w_ffn_down[1]),
             row(ln_ffn_post[1]), 256)
    return h.reshape(b, s, d)
```

```python
import math

import numpy as np
import jax
import jax.numpy as jnp
from jax import lax
from jax.experimental import pallas as pl
from jax.experimental.pallas import tpu as pltpu

F32 = jnp.float32
BF16 = jnp.bfloat16

D_MODEL = 1024
HEAD_DIM = 64
MEM_HEADS = 4
MEM_WIDTH = MEM_HEADS * HEAD_DIM
MAIN_WIDTH = D_MODEL - MEM_WIDTH
CHUNK = 128
A_GROUPS = 6
FOX_HEADS = MAIN_WIDTH // HEAD_DIM
RMS_EPS = 1e-6
LN_EPS = 1e-5
SCALE = HEAD_DIM ** -0.5
LOG2E = math.log2(math.e)
LANES = 128
VMEM_LIMIT = 56 * 1024 * 1024

N_PIECES = 3
CT_ROW = 36
V_ROWS = 80
FOX_TK = 256
FOX_TQ = 512

NT_DIMS = (((1,), (1,)), ((), ()))


def _dot(a, b):
    return jnp.dot(a, b, preferred_element_type=F32)


def _dot_nt(a, b):
    return lax.dot_general(a, b, NT_DIMS, preferred_element_type=F32)


def _rms(x, g):
    return (x * lax.rsqrt(jnp.mean(x * x, axis=-1, keepdims=True) + RMS_EPS)) * g


def _split_bf16(x):
    p1 = x.astype(BF16)
    r1 = x - p1.astype(F32)
    p2 = r1.astype(BF16)
    p3 = (r1 - p2.astype(F32)).astype(BF16)
    return p1, p2, p3


def _params(*sem):
    return pltpu.CompilerParams(dimension_semantics=sem, vmem_limit_bytes=VMEM_LIMIT)


def _full(shape):
    n = len(shape)
    return pl.BlockSpec(shape, lambda *_: (0,) * n)


def _memkv_kernel(mem_ref, g_ref, w_ref, o_ref):
    n = _rms(mem_ref[0], g_ref[0])
    o_ref[0, 0] = _dot(n.astype(BF16), w_ref[0]).astype(BF16)


def _memkv(mem, ln_mem, w_mem_kv):
    depth = ln_mem.shape[0]
    b, m, d = mem.shape
    n_out = w_mem_kv.shape[-1]
    return pl.pallas_call(
        _memkv_kernel,
        grid=(depth, b),
        in_specs=[
            pl.BlockSpec((1, m, d), lambda l, i: (i, 0, 0)),
            pl.BlockSpec((1, 1, d), lambda l, i: (l, 0, 0)),
            pl.BlockSpec((1, d, n_out), lambda l, i: (l, 0, 0)),
        ],
        out_specs=pl.BlockSpec((1, 1, m, n_out), lambda l, i: (l, i, 0, 0)),
        out_shape=jax.ShapeDtypeStruct((depth, b, m, n_out), BF16),
        compiler_params=_params("arbitrary", "arbitrary"),
        name="memkv",
    )(mem, ln_mem.reshape(depth, 1, d), w_mem_kv)


def _memory_attention(qm, kv):
    k = kv[:, :MEM_WIDTH]
    v = kv[:, MEM_WIDTH:]
    head_of_lane = lax.broadcasted_iota(jnp.int32, (1, MEM_WIDTH), 1) // HEAD_DIM
    out = jnp.zeros(qm.shape, F32)
    for hh in range(MEM_HEADS):
        sel = head_of_lane == hh
        s = _dot_nt(jnp.where(sel, qm, jnp.zeros_like(qm)), k)
        p = jnp.exp(s - jnp.max(s, axis=-1, keepdims=True))
        l = jnp.sum(p, axis=-1, keepdims=True)
        o = _dot(p.astype(BF16), v)
        out = jnp.where(sel, o / l, out)
    return out


def _mixer_a_kernel(h_ref, kv_ref, g_pre_ref, w_in_ref, wsp_ref, bsp_ref, lng_ref, lnb_ref,
                    w_out_ref, g_post_ref, o_ref, mixed_ref):
    tm = h_ref.shape[0]
    h = h_ref[...]
    a = _rms(h, g_pre_ref[...]).astype(BF16)
    proj = _dot(a, w_in_ref[...])
    u = jax.nn.gelu(proj[:, :MAIN_WIDTH])
    v = jax.nn.gelu(proj[:, MAIN_WIDTH:2 * MAIN_WIDTH])
    mu = jnp.mean(v, axis=-1, keepdims=True)
    vc = v - mu
    vn = vc * lax.rsqrt(jnp.mean(vc * vc, axis=-1, keepdims=True) + LN_EPS)
    vn = (vn * lng_ref[...] + lnb_ref[...]).astype(BF16)
    row = lax.broadcasted_iota(jnp.int32, (CHUNK, CHUNK), 0)
    col = lax.broadcasted_iota(jnp.int32, (CHUNK, CHUNK), 1)
    causal = row >= col
    for g in range(A_GROUPS):
        w = jnp.where(causal, wsp_ref[g], 0.0).astype(BF16)
        bias = bsp_ref[g]
        cols = slice(g * CHUNK, (g + 1) * CHUNK)
        for c in range(tm // CHUNK):
            rows = slice(c * CHUNK, (c + 1) * CHUNK)
            s = _dot(w, vn[rows, cols]) + bias
            mixed_ref[rows, cols] = (u[rows, cols] * s).astype(BF16)
    qm = (proj[:, 2 * MAIN_WIDTH:] * SCALE).astype(BF16)
    mixed_ref[:, MAIN_WIDTH:] = _memory_attention(qm, kv_ref[0]).astype(BF16)
    y = _dot(mixed_ref[...], w_out_ref[...])
    o_ref[...] = h + _rms(y, g_post_ref[...])


def _mixer_a(h, kv, g_pre, w_in, wsp, bsp, lng, lnb, w_out, g_post, seq, tm):
    t, d = h.shape
    per_batch = seq // tm
    return pl.pallas_call(
        _mixer_a_kernel,
        grid=(t // tm,),
        in_specs=[
            pl.BlockSpec((tm, d), lambda i: (i, 0)),
            pl.BlockSpec((1,) + kv.shape[1:], lambda i: (i // per_batch, 0, 0)),
            _full(g_pre.shape), _full(w_in.shape), _full(wsp.shape), _full(bsp.shape),
            _full(lng.shape), _full(lnb.shape), _full(w_out.shape), _full(g_post.shape),
        ],
        out_specs=pl.BlockSpec((tm, d), lambda i: (i, 0)),
        out_shape=jax.ShapeDtypeStruct((t, d), F32),
        scratch_shapes=[pltpu.VMEM((tm, d), BF16)],
        compiler_params=_params("arbitrary"),
        name="mixer_a",
    )(h, kv, g_pre, w_in, wsp, bsp, lng, lnb, w_out, g_post)


def _ffn_kernel(h_ref, g_pre_ref, wg_ref, wu_ref, wd_ref, g_post_ref, o_ref):
    h = h_ref[...]
    f = _rms(h, g_pre_ref[...]).astype(BF16)
    gate = _dot(f, wg_ref[...])
    up = _dot(f, wu_ref[...])
    act = (jax.nn.silu(gate) * up).astype(BF16)
    y = _dot(act, wd_ref[...])
    o_ref[...] = h + _rms(y, g_post_ref[...])


def _ffn(h, g_pre, wg, wu, wd, g_post, tm):
    t, d = h.shape
    return pl.pallas_call(
        _ffn_kernel,
        grid=(t // tm,),
        in_specs=[
            pl.BlockSpec((tm, d), lambda i: (i, 0)),
            _full(g_pre.shape), _full(wg.shape), _full(wu.shape), _full(wd.shape),
            _full(g_post.shape),
        ],
        out_specs=pl.BlockSpec((tm, d), lambda i: (i, 0)),
        out_shape=jax.ShapeDtypeStruct((t, d), F32),
        compiler_params=_params("arbitrary"),
        name="ffn",
    )(h, g_pre, wg, wu, wd, g_post)


def _gate_selectors():
    g = np.zeros((N_PIECES, LANES, LANES), np.float32)
    for i in range(N_PIECES):
        for hd in range(FOX_HEADS):
            g[i, hd, HEAD_DIM + N_PIECES * hd + i] = -1.0
    return jnp.asarray(g, dtype=BF16)


def _proj_b_kernel(h_ref, g_sh_ref, g_pre_ref, w_k_ref, w_vt_ref, w_f_ref, b_f_ref, w_qt_ref,
                   w_qm_ref, gsel_ref, kp_ref, qt_ref, vt_ref, qm_ref, carry_ref):
    tm = h_ref.shape[1]

    @pl.when(pl.program_id(1) == 0)
    def _():
        carry_ref[...] = jnp.zeros_like(carry_ref)

    h = h_ref[0]
    hn = h * lax.rsqrt(jnp.mean(h * h, axis=-1, keepdims=True) + RMS_EPS)
    s_in = (hn * g_sh_ref[...]).astype(BF16)
    a = (hn * g_pre_ref[...]).astype(BF16)
    lane = lax.broadcasted_iota(jnp.int32, (1, LANES), 1)

    log_f = jax.nn.log_sigmoid(_dot(s_in, w_f_ref[...]) + b_f_ref[...])
    log_f = jnp.where(lane < FOX_HEADS, log_f, 0.0)
    row = lax.broadcasted_iota(jnp.int32, (tm, tm), 0)
    col = lax.broadcasted_iota(jnp.int32, (tm, tm), 1)
    tri = (row >= col).astype(BF16)
    f1, f2, f3 = _split_bf16(log_f)
    c = _dot(tri, f1) + _dot(tri, f2) + _dot(tri, f3) + carry_ref[...]
    carry_ref[...] = c[tm - 1:tm, :]
    c2 = c * LOG2E

    n1, n2, n3 = _split_bf16(c2)
    ones_lanes = jnp.where((lane >= HEAD_DIM + CT_ROW) & (lane < HEAD_DIM + CT_ROW + N_PIECES),
                           1.0, 0.0)
    aug = _dot(n1, gsel_ref[0]) + _dot(n2, gsel_ref[1]) + _dot(n3, gsel_ref[2]) + ones_lanes
    k = _dot(s_in, w_k_ref[...])
    low = lane < HEAD_DIM
    for p in range(FOX_HEADS // 2):
        blk = k[:, p * LANES:(p + 1) * LANES]
        kp_ref[0, 2 * p] = jnp.where(low, blk, aug).astype(BF16)
        kp_ref[0, 2 * p + 1] = jnp.where(low, pltpu.roll(blk, HEAD_DIM, axis=1), aug).astype(BF16)

    qt = _dot_nt(w_qt_ref[...], a) * (SCALE * LOG2E)
    t1, t2, t3 = _split_bf16(c2.T)
    rid = lax.broadcasted_iota(jnp.int32, (HEAD_DIM, 1), 0)
    for hd in range(FOX_HEADS):
        sel = jnp.where((rid >= N_PIECES * hd) & (rid < N_PIECES * (hd + 1)), 1.0, 0.0)
        aug_q = jnp.where(rid == CT_ROW, t1[hd:hd + 1, :].astype(F32),
                          jnp.where(rid == CT_ROW + 1, t2[hd:hd + 1, :].astype(F32),
                                    jnp.where(rid == CT_ROW + 2, t3[hd:hd + 1, :].astype(F32),
                                              sel)))
        qt_ref[0, hd * LANES:hd * LANES + HEAD_DIM, :] = (
            qt[hd * HEAD_DIM:(hd + 1) * HEAD_DIM, :].astype(BF16))
        qt_ref[0, hd * LANES + HEAD_DIM:(hd + 1) * LANES, :] = aug_q.astype(BF16)

    vt = _dot_nt(w_vt_ref[...], s_in)
    rid_v = lax.broadcasted_iota(jnp.int32, (V_ROWS - HEAD_DIM, tm), 0)
    ones_row = jnp.where(rid_v == 0, 1.0, 0.0).astype(BF16)
    for hd in range(FOX_HEADS):
        vt_ref[0, hd, :HEAD_DIM, :] = vt[hd * HEAD_DIM:(hd + 1) * HEAD_DIM, :].astype(BF16)
        vt_ref[0, hd, HEAD_DIM:, :] = ones_row

    qm_ref[0] = (_dot(a, w_qm_ref[...]) * SCALE).astype(BF16)


def _proj_b(h3, g_sh, g_pre, w_k, w_vt, w_f, b_f, w_qt, w_qm, tm):
    b, s, d = h3.shape
    gsel = _gate_selectors()
    return pl.pallas_call(
        _proj_b_kernel,
        grid=(b, s // tm),
        in_specs=[pl.BlockSpec((1, tm, d), lambda i, j: (i, j, 0)),
                  _full(g_sh.shape), _full(g_pre.shape), _full(w_k.shape), _full(w_vt.shape),
                  _full(w_f.shape), _full(b_f.shape), _full(w_qt.shape), _full(w_qm.shape),
                  _full(gsel.shape)],
        out_specs=[
            pl.BlockSpec((1, FOX_HEADS, tm, LANES), lambda i, j: (i, 0, j, 0)),
            pl.BlockSpec((1, FOX_HEADS * LANES, tm), lambda i, j: (i, 0, j)),
            pl.BlockSpec((1, FOX_HEADS, V_ROWS, tm), lambda i, j: (i, 0, 0, j)),
            pl.BlockSpec((1, tm, MEM_WIDTH), lambda i, j: (i, j, 0)),
        ],
        out_shape=[
            jax.ShapeDtypeStruct((b, FOX_HEADS, s, LANES), BF16),
            jax.ShapeDtypeStruct((b, FOX_HEADS * LANES, s), BF16),
            jax.ShapeDtypeStruct((b, FOX_HEADS, V_ROWS, s), BF16),
            jax.ShapeDtypeStruct((b, s, MEM_WIDTH), BF16),
        ],
        scratch_shapes=[pltpu.VMEM((1, LANES), F32)],
        compiler_params=_params("arbitrary", "arbitrary"),
        name="proj_b",
    )(h3, g_sh, g_pre, w_k, w_vt, w_f, b_f, w_qt, w_qm, gsel)


def _fox_kernel(k_ref, qt_ref, vt_ref, o_ref, m_ref, acc_ref):
    tk = FOX_TK
    tq = FOX_TQ
    n_q = k_ref.shape[2] // tq
    sub = tq // tk
    key_pos = lax.broadcasted_iota(jnp.int32, (tk, tk), 0)
    qry_pos = lax.broadcasted_iota(jnp.int32, (tk, tk), 1)
    keep = key_pos <= qry_pos

    def q_tile(qi, carry):
        q0 = pl.multiple_of(qi * tq, tq)

        def step(hh, k0, lo, first, diagonal):
            qt = qt_ref[0, hh * LANES:(hh + 1) * LANES, pl.ds(q0 + lo, tq - lo)]
            s = _dot(k_ref[0, hh, pl.ds(k0, tk), :], qt)
            if diagonal:
                tri = jnp.where(keep, s[:, :tk], -jnp.inf)
                s = tri if tq - lo == tk else jnp.concatenate([tri, s[:, tk:]], axis=1)
            m_blk = jnp.max(s, axis=0, keepdims=True)
            v_blk = vt_ref[0, hh, :, pl.ds(k0, tk)]
            if first:
                m_ref[hh] = m_blk
                acc_ref[hh] = _dot(v_blk, jnp.exp2(s - m_blk).astype(BF16))
                return
            m_old = m_ref[hh, :, lo:]
            m_new = jnp.maximum(m_old, m_blk)
            alpha = jnp.exp2(m_old - m_new)
            p = jnp.exp2(s - m_new).astype(BF16)
            acc_ref[hh, :, lo:] = alpha * acc_ref[hh, :, lo:] + _dot(v_blk, p)
            m_ref[hh, :, lo:] = m_new

        for d in range(sub):
            for hh in range(2):
                step(hh, q0 + d * tk, d * tk, d == 0, True)

        def body(j, c):
            k0 = pl.multiple_of(j * tk, tk)
            for hh in range(2):
                step(hh, k0, 0, False, False)
            return c

        lax.fori_loop(0, qi * sub, body, 0)
        halves = [acc_ref[hh, :HEAD_DIM, :] / acc_ref[hh, HEAD_DIM:HEAD_DIM + 1, :]
                  for hh in range(2)]
        o_ref[0, pl.ds(q0, tq), :] = jnp.concatenate(halves, axis=0).T.astype(BF16)
        return carry

    lax.fori_loop(0, n_q, q_tile, 0)


def _fox_attention(kp, qt, vt):
    b, _, s, _ = kp.shape
    pairs = FOX_HEADS // 2
    return pl.pallas_call(
        _fox_kernel,
        grid=(b, pairs),
        in_specs=[
            pl.BlockSpec((1, 2, s, LANES), lambda i, p: (i, p, 0, 0)),
            pl.BlockSpec((1, 2 * LANES, s), lambda i, p: (i, p, 0)),
            pl.BlockSpec((1, 2, V_ROWS, s), lambda i, p: (i, p, 0, 0)),
        ],
        out_specs=pl.BlockSpec((1, s, LANES), lambda i, p: (i, 0, p)),
        out_shape=jax.ShapeDtypeStruct((b, s, MAIN_WIDTH), BF16),
        scratch_shapes=[pltpu.VMEM((2, 1, FOX_TQ), F32), pltpu.VMEM((2, V_ROWS, FOX_TQ), F32)],
        compiler_params=_params("arbitrary", "arbitrary"),
        name="fox_attention",
    )(kp, qt, vt)


def _mixer_b_kernel(h_ref, attn_ref, qm_ref, kv_ref, w_out_ref, g_post_ref, o_ref):
    h = h_ref[...]
    mem_o = _memory_attention(qm_ref[...], kv_ref[0]).astype(BF16)
    y = _dot(attn_ref[...], w_out_ref[:MAIN_WIDTH, :]) + _dot(mem_o, w_out_ref[MAIN_WIDTH:, :])
    o_ref[...] = h + _rms(y, g_post_ref[...])


def _mixer_b(h, attn, qm, kv, w_out, g_post, seq, tm):
    t, d = h.shape
    per_batch = seq // tm
    tile = lambda w: pl.BlockSpec((tm, w), lambda i: (i, 0))
    return pl.pallas_call(
        _mixer_b_kernel,
        grid=(t // tm,),
        in_specs=[tile(d), tile(MAIN_WIDTH), tile(MEM_WIDTH),
                  pl.BlockSpec((1,) + kv.shape[1:], lambda i: (i // per_batch, 0, 0)),
                  _full(w_out.shape), _full(g_post.shape)],
        out_specs=tile(d),
        out_shape=jax.ShapeDtypeStruct((t, d), F32),
        compiler_params=_params("arbitrary"),
        name="mixer_b",
    )(h, attn, qm, kv, w_out, g_post)


def kernel(x, mem, ln_mix_pre, ln_mix_post, ln_ffn_pre, ln_ffn_post, ln_mem, w_mem_kv, w_out,
           w_ffn_gate, w_ffn_up, w_ffn_down, w_in_a, w_spatial, b_spatial, ln_v_g, ln_v_b,
           ln_shared, w_shared_kv, b_forget, w_in_b):
    b, s, d = x.shape
    t = b * s
    row = lambda g: g.reshape(1, -1)
    bf = lambda w: w.astype(BF16)

    kv_mem = _memkv(mem, ln_mem, bf(w_mem_kv))
    h = x.reshape(t, d)

    h = _mixer_a(h, kv_mem[0], row(ln_mix_pre[0]), bf(w_in_a[0]), w_spatial[0],
                 b_spatial[0].reshape(A_GROUPS, CHUNK, 1), row(ln_v_g[0]), row(ln_v_b[0]),
                 bf(w_out[0]), row(ln_mix_post[0]), s, 256)
    h = _ffn(h, row(ln_ffn_pre[0]), bf(w_ffn_gate[0]), bf(w_ffn_up[0]), bf(w_ffn_down[0]),
             row(ln_ffn_post[0]), 256)

    pad = LANES - FOX_HEADS
    w_f = jnp.pad(w_shared_kv[:, 2 * MAIN_WIDTH:], ((0, 0), (0, pad)))
    b_f = jnp.pad(b_forget, (0, pad)).reshape(1, LANES)
    kp, qt, vt, qm = _proj_b(
        h.reshape(b, s, d), row(ln_shared), row(ln_mix_pre[1]),
        bf(w_shared_kv[:, :MAIN_WIDTH]), bf(w_shared_kv[:, MAIN_WIDTH:2 * MAIN_WIDTH].T),
        bf(w_f), b_f, bf(w_in_b[0][:, :MAIN_WIDTH].T), bf(w_in_b[0][:, MAIN_WIDTH:]), 512)
    attn = _fox_attention(kp, qt, vt)
    h = _mixer_b(h, attn.reshape(t, MAIN_WIDTH), qm.reshape(t, MEM_WIDTH), kv_mem[1],
                 bf(w_out[1]), row(ln_mix_post[1]), s, 256)
    h = _ffn(h, row(ln_ffn_pre[1]), bf(w_ffn_gate[1]), bf(w_ffn_up[1]), bf(w_ffn_down[1]),
             row(ln_ffn_post[1]), 256)
    return h.reshape(b, s, d)
```

```python
import math

import numpy as np
import jax
import jax.numpy as jnp
from jax import lax
from jax.experimental import pallas as pl
from jax.experimental.pallas import tpu as pltpu

F32 = jnp.float32
BF16 = jnp.bfloat16

D_MODEL = 1024
HEAD_DIM = 64
MEM_HEADS = 4
MEM_WIDTH = MEM_HEADS * HEAD_DIM
MAIN_WIDTH = D_MODEL - MEM_WIDTH
CHUNK = 128
A_GROUPS = 6
FOX_HEADS = MAIN_WIDTH // HEAD_DIM
RMS_EPS = 1e-6
LN_EPS = 1e-5
SCALE = HEAD_DIM ** -0.5
LOG2E = math.log2(math.e)
LANES = 128
VMEM_LIMIT = 56 * 1024 * 1024

N_PIECES = 3
CT_ROW = 36
V_ROWS = 80
FOX_TK = 256
FOX_TQ = 512

NT_DIMS = (((1,), (1,)), ((), ()))


def _dot(a, b):
    return jnp.dot(a, b, preferred_element_type=F32)


def _dot_nt(a, b):
    return lax.dot_general(a, b, NT_DIMS, preferred_element_type=F32)


def _rms(x, g):
    return (x * lax.rsqrt(jnp.mean(x * x, axis=-1, keepdims=True) + RMS_EPS)) * g


def _split_bf16(x):
    p1 = x.astype(BF16)
    r1 = x - p1.astype(F32)
    p2 = r1.astype(BF16)
    p3 = (r1 - p2.astype(F32)).astype(BF16)
    return p1, p2, p3


def _params(*sem):
    return pltpu.CompilerParams(dimension_semantics=sem, vmem_limit_bytes=VMEM_LIMIT)


def _full(shape):
    n = len(shape)
    return pl.BlockSpec(shape, lambda *_: (0,) * n)


def _memkv_kernel(mem_ref, g_ref, w_ref, o_ref):
    n = _rms(mem_ref[0], g_ref[0])
    o_ref[0, 0] = _dot(n.astype(BF16), w_ref[0]).astype(BF16)


def _memkv(mem, ln_mem, w_mem_kv):
    depth = ln_mem.shape[0]
    b, m, d = mem.shape
    n_out = w_mem_kv.shape[-1]
    return pl.pallas_call(
        _memkv_kernel,
        grid=(depth, b),
        in_specs=[
            pl.BlockSpec((1, m, d), lambda l, i: (i, 0, 0)),
            pl.BlockSpec((1, 1, d), lambda l, i: (l, 0, 0)),
            pl.BlockSpec((1, d, n_out), lambda l, i: (l, 0, 0)),
        ],
        out_specs=pl.BlockSpec((1, 1, m, n_out), lambda l, i: (l, i, 0, 0)),
        out_shape=jax.ShapeDtypeStruct((depth, b, m, n_out), BF16),
        compiler_params=_params("arbitrary", "arbitrary"),
        name="memkv",
    )(mem, ln_mem.reshape(depth, 1, d), w_mem_kv)


def _memory_attention(qm, kv):
    k = kv[:, :MEM_WIDTH]
    v = kv[:, MEM_WIDTH:]
    head_of_lane = lax.broadcasted_iota(jnp.int32, (1, MEM_WIDTH), 1) // HEAD_DIM
    out = jnp.zeros(qm.shape, F32)
    for hh in range(MEM_HEADS):
        sel = head_of_lane == hh
        s = _dot_nt(jnp.where(sel, qm, jnp.zeros_like(qm)), k)
        p = jnp.exp(s - jnp.max(s, axis=-1, keepdims=True))
        l = jnp.sum(p, axis=-1, keepdims=True)
        o = _dot(p.astype(BF16), v)
        out = jnp.where(sel, o / l, out)
    return out


def _mixer_a_kernel(h_ref, kv_ref, g_pre_ref, w_in_ref, wsp_ref, bsp_ref, lng_ref, lnb_ref,
                    w_out_ref, g_post_ref, o_ref, mixed_ref):
    tm = h_ref.shape[0]
    h = h_ref[...]
    a = _rms(h, g_pre_ref[...]).astype(BF16)
    proj = _dot(a, w_in_ref[...])
    u = jax.nn.gelu(proj[:, :MAIN_WIDTH])
    v = jax.nn.gelu(proj[:, MAIN_WIDTH:2 * MAIN_WIDTH])
    mu = jnp.mean(v, axis=-1, keepdims=True)
    vc = v - mu
    vn = vc * lax.rsqrt(jnp.mean(vc * vc, axis=-1, keepdims=True) + LN_EPS)
    vn = (vn * lng_ref[...] + lnb_ref[...]).astype(BF16)
    row = lax.broadcasted_iota(jnp.int32, (CHUNK, CHUNK), 0)
    col = lax.broadcasted_iota(jnp.int32, (CHUNK, CHUNK), 1)
    causal = row >= col
    for g in range(A_GROUPS):
        w = jnp.where(causal, wsp_ref[g], 0.0).astype(BF16)
        bias = bsp_ref[g]
        cols = slice(g * CHUNK, (g + 1) * CHUNK)
        for c in range(tm // CHUNK):
            rows = slice(c * CHUNK, (c + 1) * CHUNK)
            s = _dot(w, vn[rows, cols]) + bias
            mixed_ref[rows, cols] = (u[rows, cols] * s).astype(BF16)
    qm = (proj[:, 2 * MAIN_WIDTH:] * SCALE).astype(BF16)
    mixed_ref[:, MAIN_WIDTH:] = _memory_attention(qm, kv_ref[0]).astype(BF16)
    y = _dot(mixed_ref[...], w_out_ref[...])
    o_ref[...] = h + _rms(y, g_post_ref[...])


def _mixer_a(h, kv, g_pre, w_in, wsp, bsp, lng, lnb, w_out, g_post, seq, tm):
    t, d = h.shape
    per_batch = seq // tm
    return pl.pallas_call(
        _mixer_a_kernel,
        grid=(t // tm,),
        in_specs=[
            pl.BlockSpec((tm, d), lambda i: (i, 0)),
            pl.BlockSpec((1,) + kv.shape[1:], lambda i: (i // per_batch, 0, 0)),
            _full(g_pre.shape), _full(w_in.shape), _full(wsp.shape), _full(bsp.shape),
            _full(lng.shape), _full(lnb.shape), _full(w_out.shape), _full(g_post.shape),
        ],
        out_specs=pl.BlockSpec((tm, d), lambda i: (i, 0)),
        out_shape=jax.ShapeDtypeStruct((t, d), F32),
        scratch_shapes=[pltpu.VMEM((tm, d), BF16)],
        compiler_params=_params("arbitrary"),
        name="mixer_a",
    )(h, kv, g_pre, w_in, wsp, bsp, lng, lnb, w_out, g_post)


def _ffn_kernel(h_ref, g_pre_ref, wg_ref, wu_ref, wd_ref, g_post_ref, o_ref):
    h = h_ref[...]
    f = _rms(h, g_pre_ref[...]).astype(BF16)
    gate = _dot(f, wg_ref[...])
    up = _dot(f, wu_ref[...])
    act = (jax.nn.silu(gate) * up).astype(BF16)
    y = _dot(act, wd_ref[...])
    o_ref[...] = h + _rms(y, g_post_ref[...])


def _ffn(h, g_pre, wg, wu, wd, g_post, tm):
    t, d = h.shape
    return pl.pallas_call(
        _ffn_kernel,
        grid=(t // tm,),
        in_specs=[
            pl.BlockSpec((tm, d), lambda i: (i, 0)),
            _full(g_pre.shape), _full(wg.shape), _full(wu.shape), _full(wd.shape),
            _full(g_post.shape),
        ],
        out_specs=pl.BlockSpec((tm, d), lambda i: (i, 0)),
        out_shape=jax.ShapeDtypeStruct((t, d), F32),
        compiler_params=_params("arbitrary"),
        name="ffn",
    )(h, g_pre, wg, wu, wd, g_post)


def _gate_selectors():
    g = np.zeros((N_PIECES, LANES, LANES), np.float32)
    for i in range(N_PIECES):
        for hd in range(FOX_HEADS):
            g[i, hd, HEAD_DIM + N_PIECES * hd + i] = -1.0
    return jnp.asarray(g, dtype=BF16)


def _proj_b_kernel(h_ref, g_sh_ref, g_pre_ref, w_k_ref, w_vt_ref, w_f_ref, b_f_ref, w_qt_ref,
                   w_qm_ref, gsel_ref, kp_ref, qt_ref, vt_ref, qm_ref, carry_ref):
    tm = h_ref.shape[1]

    @pl.when(pl.program_id(1) == 0)
    def _():
        carry_ref[...] = jnp.zeros_like(carry_ref)

    h = h_ref[0]
    hn = h * lax.rsqrt(jnp.mean(h * h, axis=-1, keepdims=True) + RMS_EPS)
    s_in = (hn * g_sh_ref[...]).astype(BF16)
    a = (hn * g_pre_ref[...]).astype(BF16)
    lane = lax.broadcasted_iota(jnp.int32, (1, LANES), 1)

    log_f = jax.nn.log_sigmoid(_dot(s_in, w_f_ref[...]) + b_f_ref[...])
    log_f = jnp.where(lane < FOX_HEADS, log_f, 0.0)
    row = lax.broadcasted_iota(jnp.int32, (tm, tm), 0)
    col = lax.broadcasted_iota(jnp.int32, (tm, tm), 1)
    tri = (row >= col).astype(BF16)
    f1, f2, f3 = _split_bf16(log_f)
    c = _dot(tri, f1) + _dot(tri, f2) + _dot(tri, f3) + carry_ref[...]
    carry_ref[...] = c[tm - 1:tm, :]
    c2 = c * LOG2E

    n1, n2, n3 = _split_bf16(c2)
    ones_lanes = jnp.where((lane >= HEAD_DIM + CT_ROW) & (lane < HEAD_DIM + CT_ROW + N_PIECES),
                           1.0, 0.0)
    aug = _dot(n1, gsel_ref[0]) + _dot(n2, gsel_ref[1]) + _dot(n3, gsel_ref[2]) + ones_lanes
    k = _dot(s_in, w_k_ref[...])
    low = lane < HEAD_DIM
    for p in range(FOX_HEADS // 2):
        blk = k[:, p * LANES:(p + 1) * LANES]
        kp_ref[0, 2 * p] = jnp.where(low, blk, aug).astype(BF16)
        kp_ref[0, 2 * p + 1] = jnp.where(low, pltpu.roll(blk, HEAD_DIM, axis=1), aug).astype(BF16)

    qt = _dot_nt(w_qt_ref[...], a) * (SCALE * LOG2E)
    t1, t2, t3 = _split_bf16(c2.T)
    rid = lax.broadcasted_iota(jnp.int32, (HEAD_DIM, 1), 0)
    for hd in range(FOX_HEADS):
        sel = jnp.where((rid >= N_PIECES * hd) & (rid < N_PIECES * (hd + 1)), 1.0, 0.0)
        aug_q = jnp.where(rid == CT_ROW, t1[hd:hd + 1, :].astype(F32),
                          jnp.where(rid == CT_ROW + 1, t2[hd:hd + 1, :].astype(F32),
                                    jnp.where(rid == CT_ROW + 2, t3[hd:hd + 1, :].astype(F32),
                                              sel)))
        qt_ref[0, hd * LANES:hd * LANES + HEAD_DIM, :] = (
            qt[hd * HEAD_DIM:(hd + 1) * HEAD_DIM, :].astype(BF16))
        qt_ref[0, hd * LANES + HEAD_DIM:(hd + 1) * LANES, :] = aug_q.astype(BF16)

    vt = _dot_nt(w_vt_ref[...], s_in)
    rid_v = lax.broadcasted_iota(jnp.int32, (V_ROWS - HEAD_DIM, tm), 0)
    ones_row = jnp.where(rid_v == 0, 1.0, 0.0).astype(BF16)
    for hd in range(FOX_HEADS):
        vt_ref[0, hd, :HEAD_DIM, :] = vt[hd * HEAD_DIM:(hd + 1) * HEAD_DIM, :].astype(BF16)
        vt_ref[0, hd, HEAD_DIM:, :] = ones_row

    qm_ref[0] = (_dot(a, w_qm_ref[...]) * SCALE).astype(BF16)


def _proj_b(h3, g_sh, g_pre, w_k, w_vt, w_f, b_f, w_qt, w_qm, tm):
    b, s, d = h3.shape
    gsel = _gate_selectors()
    return pl.pallas_call(
        _proj_b_kernel,
        grid=(b, s // tm),
        in_specs=[pl.BlockSpec((1, tm, d), lambda i, j: (i, j, 0)),
                  _full(g_sh.shape), _full(g_pre.shape), _full(w_k.shape), _full(w_vt.shape),
                  _full(w_f.shape), _full(b_f.shape), _full(w_qt.shape), _full(w_qm.shape),
                  _full(gsel.shape)],
        out_specs=[
            pl.BlockSpec((1, FOX_HEADS, tm, LANES), lambda i, j: (i, 0, j, 0)),
            pl.BlockSpec((1, FOX_HEADS * LANES, tm), lambda i, j: (i, 0, j)),
            pl.BlockSpec((1, FOX_HEADS, V_ROWS, tm), lambda i, j: (i, 0, 0, j)),
            pl.BlockSpec((1, tm, MEM_WIDTH), lambda i, j: (i, j, 0)),
        ],
        out_shape=[
            jax.ShapeDtypeStruct((b, FOX_HEADS, s, LANES), BF16),
            jax.ShapeDtypeStruct((b, FOX_HEADS * LANES, s), BF16),
            jax.ShapeDtypeStruct((b, FOX_HEADS, V_ROWS, s), BF16),
            jax.ShapeDtypeStruct((b, s, MEM_WIDTH), BF16),
        ],
        scratch_shapes=[pltpu.VMEM((1, LANES), F32)],
        compiler_params=_params("arbitrary", "arbitrary"),
        name="proj_b",
    )(h3, g_sh, g_pre, w_k, w_vt, w_f, b_f, w_qt, w_qm, gsel)


def _fox_kernel(k_ref, qt_ref, vt_ref, o_ref, s0_ref, s1_ref, p0_ref, p1_ref, a0_ref, a1_ref,
                bias_ref, m_ref, acc_ref):
    tk = FOX_TK
    tq = FOX_TQ
    n_q = k_ref.shape[2] // tq
    assert tq == 2 * tk and n_q >= 2
    sub = 2
    s_refs = (s0_ref, s1_ref)
    p_refs = (p0_ref, p1_ref)
    a_refs = (a0_ref, a1_ref)

    def issue_scores(par, qi, kb):
        k0 = pl.multiple_of(kb * tk, tk)
        q0 = pl.multiple_of(qi * tq, tq)
        for hh in range(2):
            s_refs[par][hh] = _dot(k_ref[0, hh, pl.ds(k0, tk), :],
                                   qt_ref[0, hh * LANES:(hh + 1) * LANES, pl.ds(q0, tq)])

    def accumulate(par, qi, kb):
        k0 = pl.multiple_of(kb * tk, tk)
        q0 = pl.multiple_of(qi * tq, tq)
        for hh in range(2):
            pv = _dot(vt_ref[0, hh, :, pl.ds(k0, tk)], p_refs[par][hh])
            acc_ref[hh, :, pl.ds(q0, tq)] = a_refs[par][hh] * acc_ref[hh, :, pl.ds(q0, tq)] + pv

    def softmax(par, qi, mask_id):
        q0 = pl.multiple_of(qi * tq, tq)
        for hh in range(2):
            s = s_refs[par][hh]
            if mask_id is not None:
                s = s + bias_ref[mask_id]
            m_old = m_ref[hh, :, pl.ds(q0, tq)]
            m_new = jnp.maximum(m_old, jnp.max(s, axis=0, keepdims=True))
            m_ref[hh, :, pl.ds(q0, tq)] = m_new
            a_refs[par][hh] = jnp.exp2(m_old - m_new)
            p_refs[par][hh] = jnp.exp2(s - m_new).astype(BF16)

    def finalize(qi):
        q0 = pl.multiple_of(qi * tq, tq)
        halves = [acc_ref[hh, :HEAD_DIM, pl.ds(q0, tq)]
                  / acc_ref[hh, HEAD_DIM:HEAD_DIM + 1, pl.ds(q0, tq)] for hh in range(2)]
        o_ref[0, pl.ds(q0, tq), :] = jnp.concatenate(halves, axis=0).T.astype(BF16)

    def step(par, prev, cur, nxt, mask_id):
        accumulate(1 - par, *prev)
        issue_scores(1 - par, *nxt)
        softmax(par, cur[0], mask_id)

    key_pos = lax.broadcasted_iota(jnp.int32, (tk, tq), 0)
    qry_pos = lax.broadcasted_iota(jnp.int32, (tk, tq), 1)
    for d in range(sub):
        bias_ref[d] = jnp.where(key_pos + d * tk <= qry_pos, 0.0, -jnp.inf)
    m_ref[...] = jnp.full(m_ref.shape, -jnp.inf, F32)
    acc_ref[...] = jnp.zeros(acc_ref.shape, F32)
    p_refs[1][...] = jnp.zeros(p_refs[1].shape, BF16)
    a_refs[1][...] = jnp.zeros(a_refs[1].shape, F32)

    def below_next(qi, kb):
        more = kb + 1 < qi * sub
        wrap = qi + 1 < n_q
        return (jnp.where(more, qi, jnp.where(wrap, qi + 1, 0)),
                jnp.where(more, kb + 1, 0))

    issue_scores(0, 1, 0)

    def below_body(_, carry):
        prev, cur = carry[:2], carry[2:]
        nxt = below_next(*cur)
        step(0, prev, cur, nxt, None)
        nxt2 = below_next(*nxt)
        step(1, cur, nxt, nxt2, None)
        return nxt + nxt2

    n_below = sub * n_q * (n_q - 1) // 2
    assert n_below % 2 == 0
    zero = jnp.int32(0)
    carry = lax.fori_loop(0, n_below // 2, below_body, (zero, zero, jnp.int32(1), zero))

    def diag_body(qi, prev):
        kb = qi * sub
        step(0, prev, (qi, kb), (qi, kb + 1), 0)
        qn = jnp.minimum(qi + 1, n_q - 1)
        step(1, (qi, kb), (qi, kb + 1), (qn, qn * sub), 1)

        @pl.when(qi >= 1)
        def _():
            finalize(qi - 1)

        return (qi, kb + 1)

    prev = lax.fori_loop(0, n_q, diag_body, carry[:2])
    accumulate(1, *prev)
    finalize(n_q - 1)


def _fox_attention(kp, qt, vt):
    b, _, s, _ = kp.shape
    pairs = FOX_HEADS // 2
    return pl.pallas_call(
        _fox_kernel,
        grid=(b, pairs),
        in_specs=[
            pl.BlockSpec((1, 2, s, LANES), lambda i, p: (i, p, 0, 0)),
            pl.BlockSpec((1, 2 * LANES, s), lambda i, p: (i, p, 0)),
            pl.BlockSpec((1, 2, V_ROWS, s), lambda i, p: (i, p, 0, 0)),
        ],
        out_specs=pl.BlockSpec((1, s, LANES), lambda i, p: (i, 0, p)),
        out_shape=jax.ShapeDtypeStruct((b, s, MAIN_WIDTH), BF16),
        scratch_shapes=[
            pltpu.VMEM((2, FOX_TK, FOX_TQ), F32),
            pltpu.VMEM((2, FOX_TK, FOX_TQ), F32),
            pltpu.VMEM((2, FOX_TK, FOX_TQ), BF16),
            pltpu.VMEM((2, FOX_TK, FOX_TQ), BF16),
            pltpu.VMEM((2, 1, FOX_TQ), F32),
            pltpu.VMEM((2, 1, FOX_TQ), F32),
            pltpu.VMEM((2, FOX_TK, FOX_TQ), F32),
            pltpu.VMEM((2, 1, s), F32),
            pltpu.VMEM((2, V_ROWS, s), F32),
        ],
        compiler_params=_params("arbitrary", "arbitrary"),
        name="fox_attention",
    )(kp, qt, vt)


def _mixer_b_kernel(h_ref, attn_ref, qm_ref, kv_ref, w_out_ref, g_post_ref, o_ref):
    h = h_ref[...]
    mem_o = _memory_attention(qm_ref[...], kv_ref[0]).astype(BF16)
    y = _dot(attn_ref[...], w_out_ref[:MAIN_WIDTH, :]) + _dot(mem_o, w_out_ref[MAIN_WIDTH:, :])
    o_ref[...] = h + _rms(y, g_post_ref[...])


def _mixer_b(h, attn, qm, kv, w_out, g_post, seq, tm):
    t, d = h.shape
    per_batch = seq // tm
    tile = lambda w: pl.BlockSpec((tm, w), lambda i: (i, 0))
    return pl.pallas_call(
        _mixer_b_kernel,
        grid=(t // tm,),
        in_specs=[tile(d), tile(MAIN_WIDTH), tile(MEM_WIDTH),
                  pl.BlockSpec((1,) + kv.shape[1:], lambda i: (i // per_batch, 0, 0)),
                  _full(w_out.shape), _full(g_post.shape)],
        out_specs=tile(d),
        out_shape=jax.ShapeDtypeStruct((t, d), F32),
        compiler_params=_params("arbitrary"),
        name="mixer_b",
    )(h, attn, qm, kv, w_out, g_post)


def kernel(x, mem, ln_mix_pre, ln_mix_post, ln_ffn_pre, ln_ffn_post, ln_mem, w_mem_kv, w_out,
           w_ffn_gate, w_ffn_up, w_ffn_down, w_in_a, w_spatial, b_spatial, ln_v_g, ln_v_b,
           ln_shared, w_shared_kv, b_forget, w_in_b):
    b, s, d = x.shape
    t = b * s
    row = lambda g: g.reshape(1, -1)
    bf = lambda w: w.astype(BF16)

    kv_mem = _memkv(mem, ln_mem, bf(w_mem_kv))
    h = x.reshape(t, d)

    h = _mixer_a(h, kv_mem[0], row(ln_mix_pre[0]), bf(w_in_a[0]), w_spatial[0],
                 b_spatial[0].reshape(A_GROUPS, CHUNK, 1), row(ln_v_g[0]), row(ln_v_b[0]),
                 bf(w_out[0]), row(ln_mix_post[0]), s, 256)
    h = _ffn(h, row(ln_ffn_pre[0]), bf(w_ffn_gate[0]), bf(w_ffn_up[0]), bf(w_ffn_down[0]),
             row(ln_ffn_post[0]), 256)

    pad = LANES - FOX_HEADS
    w_f = jnp.pad(w_shared_kv[:, 2 * MAIN_WIDTH:], ((0, 0), (0, pad)))
    b_f = jnp.pad(b_forget, (0, pad)).reshape(1, LANES)
    kp, qt, vt, qm = _proj_b(
        h.reshape(b, s, d), row(ln_shared), row(ln_mix_pre[1]),
        bf(w_shared_kv[:, :MAIN_WIDTH]), bf(w_shared_kv[:, MAIN_WIDTH:2 * MAIN_WIDTH].T),
        bf(w_f), b_f, bf(w_in_b[0][:, :MAIN_WIDTH].T), bf(w_in_b[0][:, MAIN_WIDTH:]), 512)
    attn = _fox_attention(kp, qt, vt)
    h = _mixer_b(h, attn.reshape(t, MAIN_WIDTH), qm.reshape(t, MEM_WIDTH), kv_mem[1],
                 bf(w_out[1]), row(ln_mix_post[1]), s, 256)
    h = _ffn(h, row(ln_ffn_pre[1]), bf(w_ffn_gate[1]), bf(w_ffn_up[1]), bf(w_ffn_down[1]),
             row(ln_ffn_post[1]), 256)
    return h.reshape(b, s, d)
```

```python
import math

import numpy as np
import jax
import jax.numpy as jnp
from jax import lax
from jax.experimental import pallas as pl
from jax.experimental.pallas import tpu as pltpu

F32 = jnp.float32
BF16 = jnp.bfloat16

D_MODEL = 1024
HEAD_DIM = 64
MEM_HEADS = 4
MEM_WIDTH = MEM_HEADS * HEAD_DIM
MAIN_WIDTH = D_MODEL - MEM_WIDTH
CHUNK = 128
A_GROUPS = 6
FOX_HEADS = MAIN_WIDTH // HEAD_DIM
RMS_EPS = 1e-6
LN_EPS = 1e-5
SCALE = HEAD_DIM ** -0.5
LOG2E = math.log2(math.e)
LANES = 128
VMEM_LIMIT = 56 * 1024 * 1024
TOKEN_TILE = 512

N_PIECES = 3
CT_ROW = 36
V_ROWS = 80
FOX_TK = 256
FOX_TQ = 512
BELOW_UNROLL = 4

NT_DIMS = (((1,), (1,)), ((), ()))


def _dot(a, b):
    return jnp.dot(a, b, preferred_element_type=F32)


def _dot_nt(a, b):
    return lax.dot_general(a, b, NT_DIMS, preferred_element_type=F32)


def _rms(x, g):
    return (x * lax.rsqrt(jnp.mean(x * x, axis=-1, keepdims=True) + RMS_EPS)) * g


def _split_bf16(x):
    p1 = x.astype(BF16)
    r1 = x - p1.astype(F32)
    p2 = r1.astype(BF16)
    p3 = (r1 - p2.astype(F32)).astype(BF16)
    return p1, p2, p3


def _params(*sem):
    return pltpu.CompilerParams(dimension_semantics=sem, vmem_limit_bytes=VMEM_LIMIT)


def _full(shape):
    n = len(shape)
    return pl.BlockSpec(shape, lambda *_: (0,) * n)


def _memkv_kernel(mem_ref, g_ref, w_ref, kt_ref, v_ref):
    m = mem_ref.shape[1]
    n = _rms(mem_ref[0], g_ref[0])
    kv = _dot(n.astype(BF16), w_ref[0])
    k_t = kv[:, :MEM_WIDTH].T
    v = kv[:, MEM_WIDTH:]
    head_of_row = lax.broadcasted_iota(jnp.int32, (MEM_WIDTH, 1), 0) // HEAD_DIM
    head_of_lane = lax.broadcasted_iota(jnp.int32, (1, MEM_WIDTH), 1) // HEAD_DIM
    for hh in range(MEM_HEADS):
        kt_ref[0, 0, :, hh * m:(hh + 1) * m] = jnp.where(head_of_row == hh, k_t, 0.0).astype(BF16)
        v_ref[0, 0, hh * m:(hh + 1) * m, :] = jnp.where(head_of_lane == hh, v, 0.0).astype(BF16)


def _memkv(mem, ln_mem, w_mem_kv):
    depth = ln_mem.shape[0]
    b, m, d = mem.shape
    n_out = w_mem_kv.shape[-1]
    return pl.pallas_call(
        _memkv_kernel,
        grid=(depth, b),
        in_specs=[
            pl.BlockSpec((1, m, d), lambda l, i: (i, 0, 0)),
            pl.BlockSpec((1, 1, d), lambda l, i: (l, 0, 0)),
            pl.BlockSpec((1, d, n_out), lambda l, i: (l, 0, 0)),
        ],
        out_specs=[
            pl.BlockSpec((1, 1, MEM_WIDTH, MEM_HEADS * m), lambda l, i: (l, i, 0, 0)),
            pl.BlockSpec((1, 1, MEM_HEADS * m, MEM_WIDTH), lambda l, i: (l, i, 0, 0)),
        ],
        out_shape=[
            jax.ShapeDtypeStruct((depth, b, MEM_WIDTH, MEM_HEADS * m), BF16),
            jax.ShapeDtypeStruct((depth, b, MEM_HEADS * m, MEM_WIDTH), BF16),
        ],
        compiler_params=_params("arbitrary", "arbitrary"),
        name="memkv",
    )(mem, ln_mem.reshape(depth, 1, d), w_mem_kv)


def _memory_attention(qm, kt, v):
    m = kt.shape[1] // MEM_HEADS
    s_all = _dot(qm, kt)
    probs = []
    for hh in range(MEM_HEADS):
        s = s_all[:, hh * m:(hh + 1) * m]
        p = jnp.exp(s - jnp.max(s, axis=-1, keepdims=True))
        inv = 1.0 / jnp.sum(p, axis=-1, keepdims=True)
        probs.append((p * inv).astype(BF16))
    return _dot(jnp.concatenate(probs, axis=1), v)


def _mixer_a_kernel(h_ref, kt_ref, vm_ref, g_pre_ref, w_in_ref, wsp_ref, bsp_ref, lng_ref, lnb_ref,
                    w_out_ref, g_post_ref, o_ref, mixed_ref):
    tm = h_ref.shape[0]
    h = h_ref[...]
    a = _rms(h, g_pre_ref[...]).astype(BF16)
    proj = _dot(a, w_in_ref[...])
    u = jax.nn.gelu(proj[:, :MAIN_WIDTH])
    v = jax.nn.gelu(proj[:, MAIN_WIDTH:2 * MAIN_WIDTH])
    mu = jnp.mean(v, axis=-1, keepdims=True)
    vc = v - mu
    vn = vc * lax.rsqrt(jnp.mean(vc * vc, axis=-1, keepdims=True) + LN_EPS)
    vn = (vn * lng_ref[...] + lnb_ref[...]).astype(BF16)
    row = lax.broadcasted_iota(jnp.int32, (CHUNK, CHUNK), 0)
    col = lax.broadcasted_iota(jnp.int32, (CHUNK, CHUNK), 1)
    causal = row >= col
    for g in range(A_GROUPS):
        w = jnp.where(causal, wsp_ref[g], 0.0).astype(BF16)
        bias = bsp_ref[g]
        cols = slice(g * CHUNK, (g + 1) * CHUNK)
        for c in range(tm // CHUNK):
            rows = slice(c * CHUNK, (c + 1) * CHUNK)
            s = _dot(w, vn[rows, cols]) + bias
            mixed_ref[rows, cols] = (u[rows, cols] * s).astype(BF16)
    qm = (proj[:, 2 * MAIN_WIDTH:] * SCALE).astype(BF16)
    mixed_ref[:, MAIN_WIDTH:] = _memory_attention(qm, kt_ref[0, 0], vm_ref[0, 0]).astype(BF16)
    y = _dot(mixed_ref[...], w_out_ref[...])
    o_ref[...] = h + _rms(y, g_post_ref[...])


def _mixer_a(h, kt, vm, layer, g_pre, w_in, wsp, bsp, lng, lnb, w_out, g_post, seq, tm):
    t, d = h.shape
    per_batch = seq // tm
    return pl.pallas_call(
        _mixer_a_kernel,
        grid=(t // tm,),
        in_specs=[
            pl.BlockSpec((tm, d), lambda i: (i, 0)),
            pl.BlockSpec((1, 1) + kt.shape[2:], lambda i: (layer, i // per_batch, 0, 0)),
            pl.BlockSpec((1, 1) + vm.shape[2:], lambda i: (layer, i // per_batch, 0, 0)),
            _full(g_pre.shape), _full(w_in.shape), _full(wsp.shape), _full(bsp.shape),
            _full(lng.shape), _full(lnb.shape), _full(w_out.shape), _full(g_post.shape),
        ],
        out_specs=pl.BlockSpec((tm, d), lambda i: (i, 0)),
        out_shape=jax.ShapeDtypeStruct((t, d), F32),
        scratch_shapes=[pltpu.VMEM((tm, d), BF16)],
        compiler_params=_params("arbitrary"),
        name="mixer_a",
    )(h, kt, vm, g_pre, w_in, wsp, bsp, lng, lnb, w_out, g_post)


def _ffn_kernel(h_ref, g_pre_ref, wg_ref, wu_ref, wd_ref, g_post_ref, o_ref):
    h = h_ref[...]
    f = _rms(h, g_pre_ref[...]).astype(BF16)
    gate = _dot(f, wg_ref[...])
    up = _dot(f, wu_ref[...])
    act = (jax.nn.silu(gate) * up).astype(BF16)
    y = _dot(act, wd_ref[...])
    o_ref[...] = h + _rms(y, g_post_ref[...])


def _ffn(h, g_pre, wg, wu, wd, g_post, tm):
    t, d = h.shape
    return pl.pallas_call(
        _ffn_kernel,
        grid=(t // tm,),
        in_specs=[
            pl.BlockSpec((tm, d), lambda i: (i, 0)),
            _full(g_pre.shape), _full(wg.shape), _full(wu.shape), _full(wd.shape),
            _full(g_post.shape),
        ],
        out_specs=pl.BlockSpec((tm, d), lambda i: (i, 0)),
        out_shape=jax.ShapeDtypeStruct((t, d), F32),
        compiler_params=_params("arbitrary"),
        name="ffn",
    )(h, g_pre, wg, wu, wd, g_post)


def _gate_selectors():
    g = np.zeros((N_PIECES, LANES, LANES), np.float32)
    for i in range(N_PIECES):
        for hd in range(FOX_HEADS):
            g[i, hd, HEAD_DIM + N_PIECES * hd + i] = -1.0
    return jnp.asarray(g, dtype=BF16)


def _proj_b_kernel(h_ref, g_sh_ref, g_pre_ref, w_k_ref, w_vt_ref, w_f_ref, b_f_ref, w_qt_ref,
                   w_qm_ref, gsel_ref, kp_ref, qt_ref, vt_ref, qm_ref, carry_ref):
    tm = h_ref.shape[1]

    @pl.when(pl.program_id(1) == 0)
    def _():
        carry_ref[...] = jnp.zeros_like(carry_ref)

    h = h_ref[0]
    hn = h * lax.rsqrt(jnp.mean(h * h, axis=-1, keepdims=True) + RMS_EPS)
    s_in = (hn * g_sh_ref[...]).astype(BF16)
    a = (hn * g_pre_ref[...]).astype(BF16)
    lane = lax.broadcasted_iota(jnp.int32, (1, LANES), 1)

    log_f = jax.nn.log_sigmoid(_dot(s_in, w_f_ref[...]) + b_f_ref[...])
    log_f = jnp.where(lane < FOX_HEADS, log_f, 0.0)
    row = lax.broadcasted_iota(jnp.int32, (tm, tm), 0)
    col = lax.broadcasted_iota(jnp.int32, (tm, tm), 1)
    tri = (row >= col).astype(BF16)
    f1, f2, f3 = _split_bf16(log_f)
    c = _dot(tri, f1) + _dot(tri, f2) + _dot(tri, f3) + carry_ref[...]
    carry_ref[...] = c[tm - 1:tm, :]
    c2 = c * LOG2E

    n1, n2, n3 = _split_bf16(c2)
    ones_lanes = jnp.where((lane >= HEAD_DIM + CT_ROW) & (lane < HEAD_DIM + CT_ROW + N_PIECES),
                           1.0, 0.0)
    aug = _dot(n1, gsel_ref[0]) + _dot(n2, gsel_ref[1]) + _dot(n3, gsel_ref[2]) + ones_lanes
    k = _dot(s_in, w_k_ref[...])
    low = lane < HEAD_DIM
    for p in range(FOX_HEADS // 2):
        blk = k[:, p * LANES:(p + 1) * LANES]
        kp_ref[0, 2 * p] = jnp.where(low, blk, aug).astype(BF16)
        kp_ref[0, 2 * p + 1] = jnp.where(low, pltpu.roll(blk, HEAD_DIM, axis=1), aug).astype(BF16)

    qt = _dot_nt(w_qt_ref[...], a) * (SCALE * LOG2E)
    t1, t2, t3 = _split_bf16(c2.T)
    rid = lax.broadcasted_iota(jnp.int32, (HEAD_DIM, 1), 0)
    for hd in range(FOX_HEADS):
        sel = jnp.where((rid >= N_PIECES * hd) & (rid < N_PIECES * (hd + 1)), 1.0, 0.0)
        aug_q = jnp.where(rid == CT_ROW, t1[hd:hd + 1, :].astype(F32),
                          jnp.where(rid == CT_ROW + 1, t2[hd:hd + 1, :].astype(F32),
                                    jnp.where(rid == CT_ROW + 2, t3[hd:hd + 1, :].astype(F32),
                                              sel)))
        qt_ref[0, hd * LANES:hd * LANES + HEAD_DIM, :] = (
            qt[hd * HEAD_DIM:(hd + 1) * HEAD_DIM, :].astype(BF16))
        qt_ref[0, hd * LANES + HEAD_DIM:(hd + 1) * LANES, :] = aug_q.astype(BF16)

    vt = _dot_nt(w_vt_ref[...], s_in)
    rid_v = lax.broadcasted_iota(jnp.int32, (V_ROWS - HEAD_DIM, tm), 0)
    ones_row = jnp.where(rid_v == 0, 1.0, 0.0).astype(BF16)
    for hd in range(FOX_HEADS):
        vt_ref[0, hd, :HEAD_DIM, :] = vt[hd * HEAD_DIM:(hd + 1) * HEAD_DIM, :].astype(BF16)
        vt_ref[0, hd, HEAD_DIM:, :] = ones_row

    qm_ref[0] = (_dot(a, w_qm_ref[...]) * SCALE).astype(BF16)


def _proj_b(h3, g_sh, g_pre, w_k, w_vt, w_f, b_f, w_qt, w_qm, tm):
    b, s, d = h3.shape
    gsel = _gate_selectors()
    return pl.pallas_call(
        _proj_b_kernel,
        grid=(b, s // tm),
        in_specs=[pl.BlockSpec((1, tm, d), lambda i, j: (i, j, 0)),
                  _full(g_sh.shape), _full(g_pre.shape), _full(w_k.shape), _full(w_vt.shape),
                  _full(w_f.shape), _full(b_f.shape), _full(w_qt.shape), _full(w_qm.shape),
                  _full(gsel.shape)],
        out_specs=[
            pl.BlockSpec((1, FOX_HEADS, tm, LANES), lambda i, j: (i, 0, j, 0)),
            pl.BlockSpec((1, FOX_HEADS * LANES, tm), lambda i, j: (i, 0, j)),
            pl.BlockSpec((1, FOX_HEADS, V_ROWS, tm), lambda i, j: (i, 0, 0, j)),
            pl.BlockSpec((1, tm, MEM_WIDTH), lambda i, j: (i, j, 0)),
        ],
        out_shape=[
            jax.ShapeDtypeStruct((b, FOX_HEADS, s, LANES), BF16),
            jax.ShapeDtypeStruct((b, FOX_HEADS * LANES, s), BF16),
            jax.ShapeDtypeStruct((b, FOX_HEADS, V_ROWS, s), BF16),
            jax.ShapeDtypeStruct((b, s, MEM_WIDTH), BF16),
        ],
        scratch_shapes=[pltpu.VMEM((1, LANES), F32)],
        compiler_params=_params("arbitrary", "arbitrary"),
        name="proj_b",
    )(h3, g_sh, g_pre, w_k, w_vt, w_f, b_f, w_qt, w_qm, gsel)


def _fox_kernel(k_ref, qt_ref, vt_ref, o_ref, s0_ref, s1_ref, p0_ref, p1_ref, a0_ref, a1_ref,
                bm0_ref, bm1_ref, bias_ref, m_ref, acc_ref):
    tk = FOX_TK
    tq = FOX_TQ
    n_q = k_ref.shape[2] // tq
    assert tq == 2 * tk and n_q >= 2
    sub = 2
    s_refs = (s0_ref, s1_ref)
    p_refs = (p0_ref, p1_ref)
    a_refs = (a0_ref, a1_ref)
    bm_refs = (bm0_ref, bm1_ref)

    def issue_scores(par, qi, kb, with_max):
        k0 = pl.multiple_of(kb * tk, tk)
        q0 = pl.multiple_of(qi * tq, tq)
        for hh in range(2):
            s = _dot(k_ref[0, hh, pl.ds(k0, tk), :],
                     qt_ref[0, hh * LANES:(hh + 1) * LANES, pl.ds(q0, tq)])
            s_refs[par][hh] = s
            if with_max:
                bm_refs[par][hh] = jnp.max(s, axis=0, keepdims=True)

    def accumulate(par, qi, kb):
        k0 = pl.multiple_of(kb * tk, tk)
        q0 = pl.multiple_of(qi * tq, tq)
        for hh in range(2):
            pv = _dot(vt_ref[0, hh, :, pl.ds(k0, tk)], p_refs[par][hh])
            acc_ref[hh, :, pl.ds(q0, tq)] = a_refs[par][hh] * acc_ref[hh, :, pl.ds(q0, tq)] + pv

    def softmax(par, qi, mask_id):
        q0 = pl.multiple_of(qi * tq, tq)
        for hh in range(2):
            s = s_refs[par][hh]
            if mask_id is None:
                m_blk = bm_refs[par][hh]
            else:
                s = s + bias_ref[mask_id]
                m_blk = jnp.max(s, axis=0, keepdims=True)
            m_old = m_ref[hh, :, pl.ds(q0, tq)]
            m_new = jnp.maximum(m_old, m_blk)
            m_ref[hh, :, pl.ds(q0, tq)] = m_new
            a_refs[par][hh] = jnp.exp2(m_old - m_new)
            p_refs[par][hh] = jnp.exp2(s - m_new).astype(BF16)

    def finalize(qi):
        q0 = pl.multiple_of(qi * tq, tq)
        halves = [acc_ref[hh, :HEAD_DIM, pl.ds(q0, tq)]
                  / acc_ref[hh, HEAD_DIM:HEAD_DIM + 1, pl.ds(q0, tq)] for hh in range(2)]
        o_ref[0, pl.ds(q0, tq), :] = jnp.concatenate(halves, axis=0).T.astype(BF16)

    def step(par, prev, cur, nxt, mask_id):
        accumulate(1 - par, *prev)
        issue_scores(1 - par, *nxt, with_max=mask_id is None)
        softmax(par, cur[0], mask_id)

    key_pos = lax.broadcasted_iota(jnp.int32, (tk, tq), 0)
    qry_pos = lax.broadcasted_iota(jnp.int32, (tk, tq), 1)
    for d in range(sub):
        bias_ref[d] = jnp.where(key_pos + d * tk <= qry_pos, 0.0, -jnp.inf)
    m_ref[...] = jnp.full(m_ref.shape, -jnp.inf, F32)
    acc_ref[...] = jnp.zeros(acc_ref.shape, F32)
    p_refs[1][...] = jnp.zeros(p_refs[1].shape, BF16)
    a_refs[1][...] = jnp.zeros(a_refs[1].shape, F32)

    def below_next(qi, kb):
        more = kb + 1 < qi * sub
        wrap = qi + 1 < n_q
        return (jnp.where(more, qi, jnp.where(wrap, qi + 1, 0)),
                jnp.where(more, kb + 1, 0))

    issue_scores(0, 1, 0, with_max=True)

    def below_body(_, carry):
        prev, cur = carry[:2], carry[2:]
        for par in range(BELOW_UNROLL):
            nxt = below_next(*cur)
            step(par % 2, prev, cur, nxt, None)
            prev, cur = cur, nxt
        return prev + cur

    n_below = sub * n_q * (n_q - 1) // 2
    assert BELOW_UNROLL % 2 == 0 and n_below % BELOW_UNROLL == 0
    zero = jnp.int32(0)
    carry = lax.fori_loop(0, n_below // BELOW_UNROLL, below_body,
                          (zero, zero, jnp.int32(1), zero))

    def diag_body(qi, prev):
        kb = qi * sub
        step(0, prev, (qi, kb), (qi, kb + 1), 0)
        qn = jnp.minimum(qi + 1, n_q - 1)
        step(1, (qi, kb), (qi, kb + 1), (qn, qn * sub), 1)

        @pl.when(qi >= 1)
        def _():
            finalize(qi - 1)

        return (qi, kb + 1)

    prev = lax.fori_loop(0, n_q, diag_body, carry[:2])
    accumulate(1, *prev)
    finalize(n_q - 1)


def _fox_attention(kp, qt, vt):
    b, _, s, _ = kp.shape
    pairs = FOX_HEADS // 2
    return pl.pallas_call(
        _fox_kernel,
        grid=(b, pairs),
        in_specs=[
            pl.BlockSpec((1, 2, s, LANES), lambda i, p: (i, p, 0, 0)),
            pl.BlockSpec((1, 2 * LANES, s), lambda i, p: (i, p, 0)),
            pl.BlockSpec((1, 2, V_ROWS, s), lambda i, p: (i, p, 0, 0)),
        ],
        out_specs=pl.BlockSpec((1, s, LANES), lambda i, p: (i, 0, p)),
        out_shape=jax.ShapeDtypeStruct((b, s, MAIN_WIDTH), BF16),
        scratch_shapes=[
            pltpu.VMEM((2, FOX_TK, FOX_TQ), F32),
            pltpu.VMEM((2, FOX_TK, FOX_TQ), F32),
            pltpu.VMEM((2, FOX_TK, FOX_TQ), BF16),
            pltpu.VMEM((2, FOX_TK, FOX_TQ), BF16),
            pltpu.VMEM((2, 1, FOX_TQ), F32),
            pltpu.VMEM((2, 1, FOX_TQ), F32),
            pltpu.VMEM((2, 1, FOX_TQ), F32),
            pltpu.VMEM((2, 1, FOX_TQ), F32),
            pltpu.VMEM((2, FOX_TK, FOX_TQ), F32),
            pltpu.VMEM((2, 1, s), F32),
            pltpu.VMEM((2, V_ROWS, s), F32),
        ],
        compiler_params=_params("arbitrary", "arbitrary"),
        name="fox_attention",
    )(kp, qt, vt)


def _mixer_b_kernel(h_ref, attn_ref, qm_ref, kt_ref, vm_ref, w_out_ref, g_post_ref, o_ref):
    h = h_ref[...]
    mem_o = _memory_attention(qm_ref[...], kt_ref[0, 0], vm_ref[0, 0]).astype(BF16)
    y = _dot(attn_ref[...], w_out_ref[:MAIN_WIDTH, :]) + _dot(mem_o, w_out_ref[MAIN_WIDTH:, :])
    o_ref[...] = h + _rms(y, g_post_ref[...])


def _mixer_b(h, attn, qm, kt, vm, layer, w_out, g_post, seq, tm):
    t, d = h.shape
    per_batch = seq // tm
    tile = lambda w: pl.BlockSpec((tm, w), lambda i: (i, 0))
    return pl.pallas_call(
        _mixer_b_kernel,
        grid=(t // tm,),
        in_specs=[tile(d), tile(MAIN_WIDTH), tile(MEM_WIDTH),
                  pl.BlockSpec((1, 1) + kt.shape[2:], lambda i: (layer, i // per_batch, 0, 0)),
                  pl.BlockSpec((1, 1) + vm.shape[2:], lambda i: (layer, i // per_batch, 0, 0)),
                  _full(w_out.shape), _full(g_post.shape)],
        out_specs=tile(d),
        out_shape=jax.ShapeDtypeStruct((t, d), F32),
        compiler_params=_params("arbitrary"),
        name="mixer_b",
    )(h, attn, qm, kt, vm, w_out, g_post)


def kernel(x, mem, ln_mix_pre, ln_mix_post, ln_ffn_pre, ln_ffn_post, ln_mem, w_mem_kv, w_out,
           w_ffn_gate, w_ffn_up, w_ffn_down, w_in_a, w_spatial, b_spatial, ln_v_g, ln_v_b,
           ln_shared, w_shared_kv, b_forget, w_in_b):
    b, s, d = x.shape
    t = b * s
    row = lambda g: g.reshape(1, -1)
    bf = lambda w: w.astype(BF16)

    kt_mem, v_mem = _memkv(mem, ln_mem, bf(w_mem_kv))
    h = x.reshape(t, d)

    h = _mixer_a(h, kt_mem, v_mem, 0, row(ln_mix_pre[0]), bf(w_in_a[0]), w_spatial[0],
                 b_spatial[0].reshape(A_GROUPS, CHUNK, 1), row(ln_v_g[0]), row(ln_v_b[0]),
                 bf(w_out[0]), row(ln_mix_post[0]), s, TOKEN_TILE)
    h = _ffn(h, row(ln_ffn_pre[0]), bf(w_ffn_gate[0]), bf(w_ffn_up[0]), bf(w_ffn_down[0]),
             row(ln_ffn_post[0]), TOKEN_TILE)

    pad = LANES - FOX_HEADS
    w_f = jnp.pad(w_shared_kv[:, 2 * MAIN_WIDTH:], ((0, 0), (0, pad)))
    b_f = jnp.pad(b_forget, (0, pad)).reshape(1, LANES)
    kp, qt, vt, qm = _proj_b(
        h.reshape(b, s, d), row(ln_shared), row(ln_mix_pre[1]),
        bf(w_shared_kv[:, :MAIN_WIDTH]), bf(w_shared_kv[:, MAIN_WIDTH:2 * MAIN_WIDTH].T),
        bf(w_f), b_f, bf(w_in_b[0][:, :MAIN_WIDTH].T), bf(w_in_b[0][:, MAIN_WIDTH:]), TOKEN_TILE)
    attn = _fox_attention(kp, qt, vt)
    h = _mixer_b(h, attn.reshape(t, MAIN_WIDTH), qm.reshape(t, MEM_WIDTH), kt_mem, v_mem, 1,
                 bf(w_out[1]), row(ln_mix_post[1]), s, TOKEN_TILE)
    h = _ffn(h, row(ln_ffn_pre[1]), bf(w_ffn_gate[1]), bf(w_ffn_up[1]), bf(w_ffn_down[1]),
             row(ln_ffn_post[1]), TOKEN_TILE)
    return h.reshape(b, s, d)
```

```python
import math

import numpy as np
import jax
import jax.numpy as jnp
from jax import lax
from jax.experimental import pallas as pl
from jax.experimental.pallas import tpu as pltpu

F32 = jnp.float32
BF16 = jnp.bfloat16

D_MODEL = 1024
HEAD_DIM = 64
MEM_HEADS = 4
MEM_WIDTH = MEM_HEADS * HEAD_DIM
MAIN_WIDTH = D_MODEL - MEM_WIDTH
CHUNK = 128
A_GROUPS = 6
FOX_HEADS = MAIN_WIDTH // HEAD_DIM
RMS_EPS = 1e-6
LN_EPS = 1e-5
SCALE = HEAD_DIM ** -0.5
LOG2E = math.log2(math.e)
LANES = 128
VMEM_LIMIT = 56 * 1024 * 1024
TOKEN_TILE = 512

N_PIECES = 3
PIECE_STRIDE = 16
CT_ROW = 36
V_ROWS = 80
FOX_TK = 256
FOX_TQ = 512
BELOW_UNROLL = 4

NT_DIMS = (((1,), (1,)), ((), ()))


def _dot(a, b):
    return jnp.dot(a, b, preferred_element_type=F32)


def _dot_nt(a, b):
    return lax.dot_general(a, b, NT_DIMS, preferred_element_type=F32)


def _rms(x, g):
    return (x * lax.rsqrt(jnp.mean(x * x, axis=-1, keepdims=True) + RMS_EPS)) * g


def _split_bf16(x):
    p1 = x.astype(BF16)
    r1 = x - p1.astype(F32)
    p2 = r1.astype(BF16)
    p3 = (r1 - p2.astype(F32)).astype(BF16)
    return p1, p2, p3


def _params(*sem):
    return pltpu.CompilerParams(dimension_semantics=sem, vmem_limit_bytes=VMEM_LIMIT)


def _full(shape):
    n = len(shape)
    return pl.BlockSpec(shape, lambda *_: (0,) * n)


def _layer(shape, layer):
    n = len(shape) - 1
    return pl.BlockSpec((1,) + tuple(shape[1:]), lambda *_: (layer,) + (0,) * n)


def _memkv_kernel(mem_ref, g_ref, w_ref, kt_ref, v_ref):
    m = mem_ref.shape[1]
    n = _rms(mem_ref[0], g_ref[0])
    kv = _dot(n.astype(BF16), w_ref[0])
    k_t = kv[:, :MEM_WIDTH].T
    v = kv[:, MEM_WIDTH:]
    head_of_row = lax.broadcasted_iota(jnp.int32, (MEM_WIDTH, 1), 0) // HEAD_DIM
    head_of_lane = lax.broadcasted_iota(jnp.int32, (1, MEM_WIDTH), 1) // HEAD_DIM
    for hh in range(MEM_HEADS):
        kt_ref[0, 0, :, hh * m:(hh + 1) * m] = jnp.where(head_of_row == hh, k_t, 0.0).astype(BF16)
        v_ref[0, 0, hh * m:(hh + 1) * m, :] = jnp.where(head_of_lane == hh, v, 0.0).astype(BF16)


def _memkv(mem, ln_mem, w_mem_kv):
    depth = ln_mem.shape[0]
    b, m, d = mem.shape
    n_out = w_mem_kv.shape[-1]
    return pl.pallas_call(
        _memkv_kernel,
        grid=(depth, b),
        in_specs=[
            pl.BlockSpec((1, m, d), lambda l, i: (i, 0, 0)),
            pl.BlockSpec((1, 1, d), lambda l, i: (l, 0, 0)),
            pl.BlockSpec((1, d, n_out), lambda l, i: (l, 0, 0)),
        ],
        out_specs=[
            pl.BlockSpec((1, 1, MEM_WIDTH, MEM_HEADS * m), lambda l, i: (l, i, 0, 0)),
            pl.BlockSpec((1, 1, MEM_HEADS * m, MEM_WIDTH), lambda l, i: (l, i, 0, 0)),
        ],
        out_shape=[
            jax.ShapeDtypeStruct((depth, b, MEM_WIDTH, MEM_HEADS * m), BF16),
            jax.ShapeDtypeStruct((depth, b, MEM_HEADS * m, MEM_WIDTH), BF16),
        ],
        compiler_params=_params("arbitrary", "arbitrary"),
        name="memkv",
    )(mem, ln_mem.reshape(depth, 1, d), w_mem_kv)


def _memory_attention(qm, kt, v):
    m = kt.shape[1] // MEM_HEADS
    s_all = _dot(qm, kt)
    probs = []
    for hh in range(MEM_HEADS):
        s = s_all[:, hh * m:(hh + 1) * m]
        p = jnp.exp(s - jnp.max(s, axis=-1, keepdims=True))
        inv = 1.0 / jnp.sum(p, axis=-1, keepdims=True)
        probs.append((p * inv).astype(BF16))
    return _dot(jnp.concatenate(probs, axis=1), v)


def _mixer_a_kernel(h_ref, kt_ref, vm_ref, g_pre_ref, w_in_ref, wsp_ref, bsp_ref, lng_ref, lnb_ref,
                    w_out_ref, g_post_ref, o_ref, mixed_ref):
    tm = h_ref.shape[0]
    h = h_ref[...]
    a = _rms(h, g_pre_ref[...]).astype(BF16)
    proj = _dot(a, w_in_ref[...])
    u = jax.nn.gelu(proj[:, :MAIN_WIDTH])
    v = jax.nn.gelu(proj[:, MAIN_WIDTH:2 * MAIN_WIDTH])
    mu = jnp.mean(v, axis=-1, keepdims=True)
    vc = v - mu
    vn = vc * lax.rsqrt(jnp.mean(vc * vc, axis=-1, keepdims=True) + LN_EPS)
    vn = (vn * lng_ref[...] + lnb_ref[...]).astype(BF16)
    row = lax.broadcasted_iota(jnp.int32, (CHUNK, CHUNK), 0)
    col = lax.broadcasted_iota(jnp.int32, (CHUNK, CHUNK), 1)
    causal = row >= col
    n_chunks = tm // CHUNK
    for g in range(A_GROUPS):
        w = jnp.where(causal, wsp_ref[g], 0.0).astype(BF16)
        bias = bsp_ref[g]
        cols = slice(g * CHUNK, (g + 1) * CHUNK)
        v_g = jnp.concatenate([vn[c * CHUNK:(c + 1) * CHUNK, cols] for c in range(n_chunks)], axis=1)
        s_g = _dot(w, v_g)
        for c in range(n_chunks):
            rows = slice(c * CHUNK, (c + 1) * CHUNK)
            s = s_g[:, c * CHUNK:(c + 1) * CHUNK] + bias
            mixed_ref[rows, cols] = (u[rows, cols] * s).astype(BF16)
    qm = (proj[:, 2 * MAIN_WIDTH:] * SCALE).astype(BF16)
    mixed_ref[:, MAIN_WIDTH:] = _memory_attention(qm, kt_ref[0, 0], vm_ref[0, 0]).astype(BF16)
    y = _dot(mixed_ref[...], w_out_ref[0])
    o_ref[...] = h + _rms(y, g_post_ref[...])


def _mixer_a(h, kt, vm, layer, g_pre, w_in, wsp, bsp, lng, lnb, w_out, g_post, seq, tm):
    t, d = h.shape
    per_batch = seq // tm
    return pl.pallas_call(
        _mixer_a_kernel,
        grid=(t // tm,),
        in_specs=[
            pl.BlockSpec((tm, d), lambda i: (i, 0)),
            pl.BlockSpec((1, 1) + kt.shape[2:], lambda i: (layer, i // per_batch, 0, 0)),
            pl.BlockSpec((1, 1) + vm.shape[2:], lambda i: (layer, i // per_batch, 0, 0)),
            _full(g_pre.shape), _full(w_in.shape), _full(wsp.shape), _full(bsp.shape),
            _full(lng.shape), _full(lnb.shape), _layer(w_out.shape, layer), _full(g_post.shape),
        ],
        out_specs=pl.BlockSpec((tm, d), lambda i: (i, 0)),
        out_shape=jax.ShapeDtypeStruct((t, d), F32),
        scratch_shapes=[pltpu.VMEM((tm, d), BF16)],
        compiler_params=_params("arbitrary"),
        name="mixer_a",
    )(h, kt, vm, g_pre, w_in, wsp, bsp, lng, lnb, w_out, g_post)


def _ffn_kernel(h_ref, g_pre_ref, wg_ref, wu_ref, wd_ref, g_post_ref, o_ref):
    h = h_ref[...]
    f = _rms(h, g_pre_ref[...]).astype(BF16)
    gate = _dot(f, wg_ref[0])
    up = _dot(f, wu_ref[0])
    act = (jax.nn.silu(gate) * up).astype(BF16)
    y = _dot(act, wd_ref[0])
    o_ref[...] = h + _rms(y, g_post_ref[...])


def _ffn(h, layer, g_pre, wg, wu, wd, g_post, tm):
    t, d = h.shape
    return pl.pallas_call(
        _ffn_kernel,
        grid=(t // tm,),
        in_specs=[
            pl.BlockSpec((tm, d), lambda i: (i, 0)),
            _full(g_pre.shape), _layer(wg.shape, layer), _layer(wu.shape, layer),
            _layer(wd.shape, layer), _full(g_post.shape),
        ],
        out_specs=pl.BlockSpec((tm, d), lambda i: (i, 0)),
        out_shape=jax.ShapeDtypeStruct((t, d), F32),
        compiler_params=_params("arbitrary"),
        name="ffn",
    )(h, g_pre, wg, wu, wd, g_post)


def _gate_selector():
    g = np.zeros((LANES, LANES), np.float32)
    for i in range(N_PIECES):
        for hd in range(FOX_HEADS):
            g[PIECE_STRIDE * i + hd, HEAD_DIM + N_PIECES * hd + i] = -1.0
    return jnp.asarray(g, dtype=BF16)


def _pack_pieces(pieces):
    packed = pieces[0].astype(F32)
    for i in range(1, N_PIECES):
        packed = packed + pltpu.roll(pieces[i].astype(F32), PIECE_STRIDE * i, axis=1)
    return packed.astype(BF16)


def _proj_b_kernel(h_ref, g_sh_ref, g_pre_ref, w_k_ref, w_vt_ref, w_f_ref, b_f_ref, w_qt_ref,
                   w_qm_ref, gsel_ref, kp_ref, qt_ref, vt_ref, qm_ref, carry_ref):
    tm = h_ref.shape[1]

    @pl.when(pl.program_id(1) == 0)
    def _():
        carry_ref[...] = jnp.zeros_like(carry_ref)

    h = h_ref[0]
    hn = h * lax.rsqrt(jnp.mean(h * h, axis=-1, keepdims=True) + RMS_EPS)
    s_in = (hn * g_sh_ref[...]).astype(BF16)
    a = (hn * g_pre_ref[...]).astype(BF16)
    lane = lax.broadcasted_iota(jnp.int32, (1, LANES), 1)

    log_f = jax.nn.log_sigmoid(_dot(s_in, w_f_ref[...]) + b_f_ref[...])
    log_f = jnp.where(lane < FOX_HEADS, log_f, 0.0)
    row = lax.broadcasted_iota(jnp.int32, (tm, tm), 0)
    col = lax.broadcasted_iota(jnp.int32, (tm, tm), 1)
    tri = (row >= col).astype(BF16)
    sums = _dot(tri, _pack_pieces(_split_bf16(log_f)))
    c = sums + carry_ref[...]
    for i in range(1, N_PIECES):
        c = c + pltpu.roll(sums, LANES - PIECE_STRIDE * i, axis=1)
    c = jnp.where(lane < FOX_HEADS, c, 0.0)
    carry_ref[...] = c[tm - 1:tm, :]
    c2 = c * LOG2E

    ones_lanes = jnp.where((lane >= HEAD_DIM + CT_ROW) & (lane < HEAD_DIM + CT_ROW + N_PIECES),
                           1.0, 0.0)
    aug = _dot(_pack_pieces(_split_bf16(c2)), gsel_ref[...]) + ones_lanes
    k = _dot(s_in, w_k_ref[...])
    low = lane < HEAD_DIM
    for p in range(FOX_HEADS // 2):
        blk = k[:, p * LANES:(p + 1) * LANES]
        kp_ref[0, 2 * p] = jnp.where(low, blk, aug).astype(BF16)
        kp_ref[0, 2 * p + 1] = jnp.where(low, pltpu.roll(blk, HEAD_DIM, axis=1), aug).astype(BF16)

    qt = _dot_nt(w_qt_ref[...], a) * (SCALE * LOG2E)
    t1, t2, t3 = _split_bf16(c2.T)
    rid = lax.broadcasted_iota(jnp.int32, (HEAD_DIM, 1), 0)
    for hd in range(FOX_HEADS):
        sel = jnp.where((rid >= N_PIECES * hd) & (rid < N_PIECES * (hd + 1)), 1.0, 0.0)
        aug_q = jnp.where(rid == CT_ROW, t1[hd:hd + 1, :].astype(F32),
                          jnp.where(rid == CT_ROW + 1, t2[hd:hd + 1, :].astype(F32),
                                    jnp.where(rid == CT_ROW + 2, t3[hd:hd + 1, :].astype(F32),
                                              sel)))
        qt_ref[0, hd * LANES:hd * LANES + HEAD_DIM, :] = (
            qt[hd * HEAD_DIM:(hd + 1) * HEAD_DIM, :].astype(BF16))
        qt_ref[0, hd * LANES + HEAD_DIM:(hd + 1) * LANES, :] = aug_q.astype(BF16)

    vt = _dot_nt(w_vt_ref[...], s_in)
    rid_v = lax.broadcasted_iota(jnp.int32, (V_ROWS - HEAD_DIM, tm), 0)
    ones_row = jnp.where(rid_v == 0, 1.0, 0.0).astype(BF16)
    for hd in range(FOX_HEADS):
        vt_ref[0, hd, :HEAD_DIM, :] = vt[hd * HEAD_DIM:(hd + 1) * HEAD_DIM, :].astype(BF16)
        vt_ref[0, hd, HEAD_DIM:, :] = ones_row

    qm_ref[0] = (_dot(a, w_qm_ref[...]) * SCALE).astype(BF16)


def _proj_b(h3, g_sh, g_pre, w_k, w_vt, w_f, b_f, w_qt, w_qm, tm):
    b, s, d = h3.shape
    gsel = _gate_selector()
    return pl.pallas_call(
        _proj_b_kernel,
        grid=(b, s // tm),
        in_specs=[pl.BlockSpec((1, tm, d), lambda i, j: (i, j, 0)),
                  _full(g_sh.shape), _full(g_pre.shape), _full(w_k.shape), _full(w_vt.shape),
                  _full(w_f.shape), _full(b_f.shape), _full(w_qt.shape), _full(w_qm.shape),
                  _full(gsel.shape)],
        out_specs=[
            pl.BlockSpec((1, FOX_HEADS, tm, LANES), lambda i, j: (i, 0, j, 0)),
            pl.BlockSpec((1, FOX_HEADS * LANES, tm), lambda i, j: (i, 0, j)),
            pl.BlockSpec((1, FOX_HEADS, V_ROWS, tm), lambda i, j: (i, 0, 0, j)),
            pl.BlockSpec((1, tm, MEM_WIDTH), lambda i, j: (i, j, 0)),
        ],
        out_shape=[
            jax.ShapeDtypeStruct((b, FOX_HEADS, s, LANES), BF16),
            jax.ShapeDtypeStruct((b, FOX_HEADS * LANES, s), BF16),
            jax.ShapeDtypeStruct((b, FOX_HEADS, V_ROWS, s), BF16),
            jax.ShapeDtypeStruct((b, s, MEM_WIDTH), BF16),
        ],
        scratch_shapes=[pltpu.VMEM((1, LANES), F32)],
        compiler_params=_params("arbitrary", "arbitrary"),
        name="proj_b",
    )(h3, g_sh, g_pre, w_k, w_vt, w_f, b_f, w_qt, w_qm, gsel)


def _fox_kernel(k_ref, qt_ref, vt_ref, o_ref, s0_ref, s1_ref, p0_ref, p1_ref, a0_ref, a1_ref,
                bm0_ref, bm1_ref, bias_ref, m_ref, acc_ref):
    tk = FOX_TK
    tq = FOX_TQ
    n_q = k_ref.shape[2] // tq
    assert tq == 2 * tk and n_q >= 2
    sub = 2
    s_refs = (s0_ref, s1_ref)
    p_refs = (p0_ref, p1_ref)
    a_refs = (a0_ref, a1_ref)
    bm_refs = (bm0_ref, bm1_ref)

    def issue_scores(par, qi, kb, with_max):
        k0 = pl.multiple_of(kb * tk, tk)
        q0 = pl.multiple_of(qi * tq, tq)
        for hh in range(2):
            s = _dot(k_ref[0, hh, pl.ds(k0, tk), :],
                     qt_ref[0, hh * LANES:(hh + 1) * LANES, pl.ds(q0, tq)])
            s_refs[par][hh] = s
            if with_max:
                bm_refs[par][hh] = jnp.max(s, axis=0, keepdims=True)

    def accumulate(par, qi, kb):
        k0 = pl.multiple_of(kb * tk, tk)
        q0 = pl.multiple_of(qi * tq, tq)
        for hh in range(2):
            pv = _dot(vt_ref[0, hh, :, pl.ds(k0, tk)], p_refs[par][hh])
            acc_ref[hh, :, pl.ds(q0, tq)] = a_refs[par][hh] * acc_ref[hh, :, pl.ds(q0, tq)] + pv

    def softmax(par, qi, mask_id):
        q0 = pl.multiple_of(qi * tq, tq)
        for hh in range(2):
            s = s_refs[par][hh]
            if mask_id is None:
                m_blk = bm_refs[par][hh]
            else:
                s = s + bias_ref[mask_id]
                m_blk = jnp.max(s, axis=0, keepdims=True)
            m_old = m_ref[hh, :, pl.ds(q0, tq)]
            m_new = jnp.maximum(m_old, m_blk)
            m_ref[hh, :, pl.ds(q0, tq)] = m_new
            a_refs[par][hh] = jnp.exp2(m_old - m_new)
            p_refs[par][hh] = jnp.exp2(s - m_new).astype(BF16)

    def finalize(qi):
        q0 = pl.multiple_of(qi * tq, tq)
        halves = [acc_ref[hh, :HEAD_DIM, pl.ds(q0, tq)]
                  / acc_ref[hh, HEAD_DIM:HEAD_DIM + 1, pl.ds(q0, tq)] for hh in range(2)]
        o_ref[0, pl.ds(q0, tq), :] = jnp.concatenate(halves, axis=0).T.astype(BF16)

    def step(par, prev, cur, nxt, mask_id):
        accumulate(1 - par, *prev)
        issue_scores(1 - par, *nxt, with_max=mask_id is None)
        softmax(par, cur[0], mask_id)

    key_pos = lax.broadcasted_iota(jnp.int32, (tk, tq), 0)
    qry_pos = lax.broadcasted_iota(jnp.int32, (tk, tq), 1)
    for d in range(sub):
        bias_ref[d] = jnp.where(key_pos + d * tk <= qry_pos, 0.0, -jnp.inf)
    m_ref[...] = jnp.full(m_ref.shape, -jnp.inf, F32)
    acc_ref[...] = jnp.zeros(acc_ref.shape, F32)
    p_refs[1][...] = jnp.zeros(p_refs[1].shape, BF16)
    a_refs[1][...] = jnp.zeros(a_refs[1].shape, F32)

    def below_next(qi, kb):
        more = kb + 1 < qi * sub
        wrap = qi + 1 < n_q
        return (jnp.where(more, qi, jnp.where(wrap, qi + 1, 0)),
                jnp.where(more, kb + 1, 0))

    issue_scores(0, 1, 0, with_max=True)

    def below_body(_, carry):
        prev, cur = carry[:2], carry[2:]
        for par in range(BELOW_UNROLL):
            nxt = below_next(*cur)
            step(par % 2, prev, cur, nxt, None)
            prev, cur = cur, nxt
        return prev + cur

    n_below = sub * n_q * (n_q - 1) // 2
    assert BELOW_UNROLL % 2 == 0 and n_below % BELOW_UNROLL == 0
    zero = jnp.int32(0)
    carry = lax.fori_loop(0, n_below // BELOW_UNROLL, below_body,
                          (zero, zero, jnp.int32(1), zero))

    def diag_body(qi, prev):
        kb = qi * sub
        step(0, prev, (qi, kb), (qi, kb + 1), 0)
        qn = jnp.minimum(qi + 1, n_q - 1)
        step(1, (qi, kb), (qi, kb + 1), (qn, qn * sub), 1)

        @pl.when(qi >= 1)
        def _():
            finalize(qi - 1)

        return (qi, kb + 1)

    prev = lax.fori_loop(0, n_q, diag_body, carry[:2])
    accumulate(1, *prev)
    finalize(n_q - 1)


def _fox_attention(kp, qt, vt):
    b, _, s, _ = kp.shape
    pairs = FOX_HEADS // 2
    return pl.pallas_call(
        _fox_kernel,
        grid=(b, pairs),
        in_specs=[
            pl.BlockSpec((1, 2, s, LANES), lambda i, p: (i, p, 0, 0)),
            pl.BlockSpec((1, 2 * LANES, s), lambda i, p: (i, p, 0)),
            pl.BlockSpec((1, 2, V_ROWS, s), lambda i, p: (i, p, 0, 0)),
        ],
        out_specs=pl.BlockSpec((1, s, LANES), lambda i, p: (i, 0, p)),
        out_shape=jax.ShapeDtypeStruct((b, s, MAIN_WIDTH), BF16),
        scratch_shapes=[
            pltpu.VMEM((2, FOX_TK, FOX_TQ), F32),
            pltpu.VMEM((2, FOX_TK, FOX_TQ), F32),
            pltpu.VMEM((2, FOX_TK, FOX_TQ), BF16),
            pltpu.VMEM((2, FOX_TK, FOX_TQ), BF16),
            pltpu.VMEM((2, 1, FOX_TQ), F32),
            pltpu.VMEM((2, 1, FOX_TQ), F32),
            pltpu.VMEM((2, 1, FOX_TQ), F32),
            pltpu.VMEM((2, 1, FOX_TQ), F32),
            pltpu.VMEM((2, FOX_TK, FOX_TQ), F32),
            pltpu.VMEM((2, 1, s), F32),
            pltpu.VMEM((2, V_ROWS, s), F32),
        ],
        compiler_params=_params("arbitrary", "arbitrary"),
        name="fox_attention",
    )(kp, qt, vt)


def _mixer_b_kernel(h_ref, attn_ref, qm_ref, kt_ref, vm_ref, w_out_ref, g_post_ref, o_ref):
    h = h_ref[...]
    mem_o = _memory_attention(qm_ref[...], kt_ref[0, 0], vm_ref[0, 0]).astype(BF16)
    y = _dot(attn_ref[...], w_out_ref[0, :MAIN_WIDTH, :]) + _dot(mem_o, w_out_ref[0, MAIN_WIDTH:, :])
    o_ref[...] = h + _rms(y, g_post_ref[...])


def _mixer_b(h, attn, qm, kt, vm, layer, w_out, g_post, seq, tm):
    t, d = h.shape
    per_batch = seq // tm
    tile = lambda w: pl.BlockSpec((tm, w), lambda i: (i, 0))
    return pl.pallas_call(
        _mixer_b_kernel,
        grid=(t // tm,),
        in_specs=[tile(d), tile(MAIN_WIDTH), tile(MEM_WIDTH),
                  pl.BlockSpec((1, 1) + kt.shape[2:], lambda i: (layer, i // per_batch, 0, 0)),
                  pl.BlockSpec((1, 1) + vm.shape[2:], lambda i: (layer, i // per_batch, 0, 0)),
                  _layer(w_out.shape, layer), _full(g_post.shape)],
        out_specs=tile(d),
        out_shape=jax.ShapeDtypeStruct((t, d), F32),
        compiler_params=_params("arbitrary"),
        name="mixer_b",
    )(h, attn, qm, kt, vm, w_out, g_post)


def kernel(x, mem, ln_mix_pre, ln_mix_post, ln_ffn_pre, ln_ffn_post, ln_mem, w_mem_kv, w_out,
           w_ffn_gate, w_ffn_up, w_ffn_down, w_in_a, w_spatial, b_spatial, ln_v_g, ln_v_b,
           ln_shared, w_shared_kv, b_forget, w_in_b):
    b, s, d = x.shape
    t = b * s
    row = lambda g: g.reshape(1, -1)
    bf = lambda w: w.astype(BF16)

    w_out_bf, wg_bf, wu_bf, wd_bf = bf(w_out), bf(w_ffn_gate), bf(w_ffn_up), bf(w_ffn_down)
    kt_mem, v_mem = _memkv(mem, ln_mem, bf(w_mem_kv))
    h = x.reshape(t, d)

    h = _mixer_a(h, kt_mem, v_mem, 0, row(ln_mix_pre[0]), bf(w_in_a[0]), w_spatial[0],
                 b_spatial[0].reshape(A_GROUPS, CHUNK, 1), row(ln_v_g[0]), row(ln_v_b[0]),
                 w_out_bf, row(ln_mix_post[0]), s, TOKEN_TILE)
    h = _ffn(h, 0, row(ln_ffn_pre[0]), wg_bf, wu_bf, wd_bf, row(ln_ffn_post[0]), TOKEN_TILE)

    pad = LANES - FOX_HEADS
    w_f = jnp.pad(w_shared_kv[:, 2 * MAIN_WIDTH:], ((0, 0), (0, pad)))
    b_f = jnp.pad(b_forget, (0, pad)).reshape(1, LANES)
    kp, qt, vt, qm = _proj_b(
        h.reshape(b, s, d), row(ln_shared), row(ln_mix_pre[1]),
        bf(w_shared_kv[:, :MAIN_WIDTH]), bf(w_shared_kv[:, MAIN_WIDTH:2 * MAIN_WIDTH].T),
        bf(w_f), b_f, bf(w_in_b[0][:, :MAIN_WIDTH].T), bf(w_in_b[0][:, MAIN_WIDTH:]), TOKEN_TILE)
    attn = _fox_attention(kp, qt, vt)
    h = _mixer_b(h, attn.reshape(t, MAIN_WIDTH), qm.reshape(t, MEM_WIDTH), kt_mem, v_mem, 1,
                 w_out_bf, row(ln_mix_post[1]), s, TOKEN_TILE)
    h = _ffn(h, 1, row(ln_ffn_pre[1]), wg_bf, wu_bf, wd_bf, row(ln_ffn_post[1]), TOKEN_TILE)
    return h.reshape(b, s, d)
```

```python
import math

import numpy as np
import jax
import jax.numpy as jnp
from jax import lax
from jax.experimental import pallas as pl
from jax.experimental.pallas import tpu as pltpu

F32 = jnp.float32
BF16 = jnp.bfloat16

D_MODEL = 1024
HEAD_DIM = 64
MEM_HEADS = 4
MEM_WIDTH = MEM_HEADS * HEAD_DIM
MAIN_WIDTH = D_MODEL - MEM_WIDTH
CHUNK = 128
A_GROUPS = 6
FOX_HEADS = MAIN_WIDTH // HEAD_DIM
RMS_EPS = 1e-6
LN_EPS = 1e-5
SCALE = HEAD_DIM ** -0.5
LOG2E = math.log2(math.e)
LANES = 128
VMEM_LIMIT = 56 * 1024 * 1024
TOKEN_TILE = 512

N_PIECES = 3
PIECE_STRIDE = 16
CT_ROW = 36
V_ROWS = 80
FOX_TK = 256
FOX_TQ = 512
STEPS_PER_ITER = 4

NT_DIMS = (((1,), (1,)), ((), ()))


def _dot(a, b):
    return jnp.dot(a, b, preferred_element_type=F32)


def _dot_nt(a, b):
    return lax.dot_general(a, b, NT_DIMS, preferred_element_type=F32)


def _rms(x, g):
    return (x * lax.rsqrt(jnp.mean(x * x, axis=-1, keepdims=True) + RMS_EPS)) * g


def _split_bf16(x):
    p1 = x.astype(BF16)
    r1 = x - p1.astype(F32)
    p2 = r1.astype(BF16)
    p3 = (r1 - p2.astype(F32)).astype(BF16)
    return p1, p2, p3


def _params(*sem):
    return pltpu.CompilerParams(dimension_semantics=sem, vmem_limit_bytes=VMEM_LIMIT)


def _full(shape):
    n = len(shape)
    return pl.BlockSpec(shape, lambda *_: (0,) * n)


def _layer(shape, layer):
    n = len(shape) - 1
    return pl.BlockSpec((1,) + tuple(shape[1:]), lambda *_: (layer,) + (0,) * n)


def _memkv_kernel(mem_ref, g_ref, w_ref, kt_ref, v_ref):
    m = mem_ref.shape[1]
    n = _rms(mem_ref[0], g_ref[0])
    kv = _dot(n.astype(BF16), w_ref[0])
    k_t = kv[:, :MEM_WIDTH].T
    v = kv[:, MEM_WIDTH:]
    head_of_row = lax.broadcasted_iota(jnp.int32, (MEM_WIDTH, 1), 0) // HEAD_DIM
    head_of_lane = lax.broadcasted_iota(jnp.int32, (1, MEM_WIDTH), 1) // HEAD_DIM
    for hh in range(MEM_HEADS):
        kt_ref[0, 0, :, hh * m:(hh + 1) * m] = jnp.where(head_of_row == hh, k_t, 0.0).astype(BF16)
        v_ref[0, 0, hh * m:(hh + 1) * m, :] = jnp.where(head_of_lane == hh, v, 0.0).astype(BF16)


def _memkv(mem, ln_mem, w_mem_kv):
    depth = ln_mem.shape[0]
    b, m, d = mem.shape
    n_out = w_mem_kv.shape[-1]
    return pl.pallas_call(
        _memkv_kernel,
        grid=(depth, b),
        in_specs=[
            pl.BlockSpec((1, m, d), lambda l, i: (i, 0, 0)),
            pl.BlockSpec((1, 1, d), lambda l, i: (l, 0, 0)),
            pl.BlockSpec((1, d, n_out), lambda l, i: (l, 0, 0)),
        ],
        out_specs=[
            pl.BlockSpec((1, 1, MEM_WIDTH, MEM_HEADS * m), lambda l, i: (l, i, 0, 0)),
            pl.BlockSpec((1, 1, MEM_HEADS * m, MEM_WIDTH), lambda l, i: (l, i, 0, 0)),
        ],
        out_shape=[
            jax.ShapeDtypeStruct((depth, b, MEM_WIDTH, MEM_HEADS * m), BF16),
            jax.ShapeDtypeStruct((depth, b, MEM_HEADS * m, MEM_WIDTH), BF16),
        ],
        compiler_params=_params("arbitrary", "arbitrary"),
        name="memkv",
    )(mem, ln_mem.reshape(depth, 1, d), w_mem_kv)


def _memory_attention(qm, kt, v):
    m = kt.shape[1] // MEM_HEADS
    s_all = _dot(qm, kt)
    probs = []
    for hh in range(MEM_HEADS):
        s = s_all[:, hh * m:(hh + 1) * m]
        p = jnp.exp(s - jnp.max(s, axis=-1, keepdims=True))
        inv = 1.0 / jnp.sum(p, axis=-1, keepdims=True)
        probs.append((p * inv).astype(BF16))
    return _dot(jnp.concatenate(probs, axis=1), v)


def _mixer_a_kernel(h_ref, kt_ref, vm_ref, g_pre_ref, w_in_ref, wsp_ref, bsp_ref, lng_ref, lnb_ref,
                    w_out_ref, g_post_ref, o_ref, mixed_ref):
    tm = h_ref.shape[0]
    h = h_ref[...]
    a = _rms(h, g_pre_ref[...]).astype(BF16)
    proj = _dot(a, w_in_ref[...])
    u = jax.nn.gelu(proj[:, :MAIN_WIDTH])
    v = jax.nn.gelu(proj[:, MAIN_WIDTH:2 * MAIN_WIDTH])
    mu = jnp.mean(v, axis=-1, keepdims=True)
    vc = v - mu
    vn = vc * lax.rsqrt(jnp.mean(vc * vc, axis=-1, keepdims=True) + LN_EPS)
    vn = (vn * lng_ref[...] + lnb_ref[...]).astype(BF16)
    row = lax.broadcasted_iota(jnp.int32, (CHUNK, CHUNK), 0)
    col = lax.broadcasted_iota(jnp.int32, (CHUNK, CHUNK), 1)
    causal = row >= col
    n_chunks = tm // CHUNK
    for g in range(A_GROUPS):
        w = jnp.where(causal, wsp_ref[g], 0.0).astype(BF16)
        bias = bsp_ref[g]
        cols = slice(g * CHUNK, (g + 1) * CHUNK)
        v_g = jnp.concatenate([vn[c * CHUNK:(c + 1) * CHUNK, cols] for c in range(n_chunks)], axis=1)
        s_g = _dot(w, v_g)
        for c in range(n_chunks):
            rows = slice(c * CHUNK, (c + 1) * CHUNK)
            s = s_g[:, c * CHUNK:(c + 1) * CHUNK] + bias
            mixed_ref[rows, cols] = (u[rows, cols] * s).astype(BF16)
    qm = (proj[:, 2 * MAIN_WIDTH:] * SCALE).astype(BF16)
    mixed_ref[:, MAIN_WIDTH:] = _memory_attention(qm, kt_ref[0, 0], vm_ref[0, 0]).astype(BF16)
    y = _dot(mixed_ref[...], w_out_ref[0])
    o_ref[...] = h + _rms(y, g_post_ref[...])


def _mixer_a(h, kt, vm, layer, g_pre, w_in, wsp, bsp, lng, lnb, w_out, g_post, seq, tm):
    t, d = h.shape
    per_batch = seq // tm
    return pl.pallas_call(
        _mixer_a_kernel,
        grid=(t // tm,),
        in_specs=[
            pl.BlockSpec((tm, d), lambda i: (i, 0)),
            pl.BlockSpec((1, 1) + kt.shape[2:], lambda i: (layer, i // per_batch, 0, 0)),
            pl.BlockSpec((1, 1) + vm.shape[2:], lambda i: (layer, i // per_batch, 0, 0)),
            _full(g_pre.shape), _full(w_in.shape), _full(wsp.shape), _full(bsp.shape),
            _full(lng.shape), _full(lnb.shape), _layer(w_out.shape, layer), _full(g_post.shape),
        ],
        out_specs=pl.BlockSpec((tm, d), lambda i: (i, 0)),
        out_shape=jax.ShapeDtypeStruct((t, d), F32),
        scratch_shapes=[pltpu.VMEM((tm, d), BF16)],
        compiler_params=_params("arbitrary"),
        name="mixer_a",
    )(h, kt, vm, g_pre, w_in, wsp, bsp, lng, lnb, w_out, g_post)


def _ffn_kernel(h_ref, g_pre_ref, wg_ref, wu_ref, wd_ref, g_post_ref, o_ref):
    half = h_ref.shape[0] // 2
    for r in range(2):
        rows = slice(r * half, (r + 1) * half)
        h = h_ref[rows, :]
        f = _rms(h, g_pre_ref[...]).astype(BF16)
        gate = _dot(f, wg_ref[0])
        up = _dot(f, wu_ref[0])
        act = (jax.nn.silu(gate) * up).astype(BF16)
        y = _dot(act, wd_ref[0])
        o_ref[rows, :] = h + _rms(y, g_post_ref[...])


def _ffn(h, layer, g_pre, wg, wu, wd, g_post, tm):
    t, d = h.shape
    return pl.pallas_call(
        _ffn_kernel,
        grid=(t // tm,),
        in_specs=[
            pl.BlockSpec((tm, d), lambda i: (i, 0)),
            _full(g_pre.shape), _layer(wg.shape, layer), _layer(wu.shape, layer),
            _layer(wd.shape, layer), _full(g_post.shape),
        ],
        out_specs=pl.BlockSpec((tm, d), lambda i: (i, 0)),
        out_shape=jax.ShapeDtypeStruct((t, d), F32),
        compiler_params=_params("arbitrary"),
        name="ffn",
    )(h, g_pre, wg, wu, wd, g_post)


def _gate_selector():
    g = np.zeros((LANES, LANES), np.float32)
    for i in range(N_PIECES):
        for hd in range(FOX_HEADS):
            g[PIECE_STRIDE * i + hd, HEAD_DIM + N_PIECES * hd + i] = -1.0
    return jnp.asarray(g, dtype=BF16)


def _pack_pieces(pieces):
    packed = pieces[0].astype(F32)
    for i in range(1, N_PIECES):
        packed = packed + pltpu.roll(pieces[i].astype(F32), PIECE_STRIDE * i, axis=1)
    return packed.astype(BF16)


def _proj_b_kernel(h_ref, g_sh_ref, g_pre_ref, w_k_ref, w_vt_ref, w_f_ref, b_f_ref, w_qt_ref,
                   w_qm_ref, gsel_ref, kp_ref, qt_ref, vt_ref, qm_ref, carry_ref):
    tm = h_ref.shape[1]

    @pl.when(pl.program_id(1) == 0)
    def _():
        carry_ref[...] = jnp.zeros_like(carry_ref)

    h = h_ref[0]
    hn = h * lax.rsqrt(jnp.mean(h * h, axis=-1, keepdims=True) + RMS_EPS)
    s_in = (hn * g_sh_ref[...]).astype(BF16)
    a = (hn * g_pre_ref[...]).astype(BF16)
    lane = lax.broadcasted_iota(jnp.int32, (1, LANES), 1)

    log_f = jax.nn.log_sigmoid(_dot(s_in, w_f_ref[...]) + b_f_ref[...])
    log_f = jnp.where(lane < FOX_HEADS, log_f, 0.0)
    row = lax.broadcasted_iota(jnp.int32, (tm, tm), 0)
    col = lax.broadcasted_iota(jnp.int32, (tm, tm), 1)
    tri = (row >= col).astype(BF16)
    sums = _dot(tri, _pack_pieces(_split_bf16(log_f)))
    c = sums + carry_ref[...]
    for i in range(1, N_PIECES):
        c = c + pltpu.roll(sums, LANES - PIECE_STRIDE * i, axis=1)
    c = jnp.where(lane < FOX_HEADS, c, 0.0)
    carry_ref[...] = c[tm - 1:tm, :]
    c2 = c * LOG2E

    ones_lanes = jnp.where((lane >= HEAD_DIM + CT_ROW) & (lane < HEAD_DIM + CT_ROW + N_PIECES),
                           1.0, 0.0)
    aug = _dot(_pack_pieces(_split_bf16(c2)), gsel_ref[...]) + ones_lanes
    k = _dot(s_in, w_k_ref[...])
    low = lane < HEAD_DIM
    for p in range(FOX_HEADS // 2):
        blk = k[:, p * LANES:(p + 1) * LANES]
        kp_ref[0, 2 * p] = jnp.where(low, blk, aug).astype(BF16)
        kp_ref[0, 2 * p + 1] = jnp.where(low, pltpu.roll(blk, HEAD_DIM, axis=1), aug).astype(BF16)

    qt = _dot_nt(w_qt_ref[...], a) * (SCALE * LOG2E)
    t1, t2, t3 = _split_bf16(c2.T)
    rid = lax.broadcasted_iota(jnp.int32, (HEAD_DIM, 1), 0)
    for hd in range(FOX_HEADS):
        sel = jnp.where((rid >= N_PIECES * hd) & (rid < N_PIECES * (hd + 1)), 1.0, 0.0)
        aug_q = jnp.where(rid == CT_ROW, t1[hd:hd + 1, :].astype(F32),
                          jnp.where(rid == CT_ROW + 1, t2[hd:hd + 1, :].astype(F32),
                                    jnp.where(rid == CT_ROW + 2, t3[hd:hd + 1, :].astype(F32),
                                              sel)))
        qt_ref[0, hd * LANES:hd * LANES + HEAD_DIM, :] = (
            qt[hd * HEAD_DIM:(hd + 1) * HEAD_DIM, :].astype(BF16))
        qt_ref[0, hd * LANES + HEAD_DIM:(hd + 1) * LANES, :] = aug_q.astype(BF16)

    vt = _dot_nt(w_vt_ref[...], s_in)
    rid_v = lax.broadcasted_iota(jnp.int32, (V_ROWS - HEAD_DIM, tm), 0)
    ones_row = jnp.where(rid_v == 0, 1.0, 0.0).astype(BF16)
    for hd in range(FOX_HEADS):
        vt_ref[0, hd, :HEAD_DIM, :] = vt[hd * HEAD_DIM:(hd + 1) * HEAD_DIM, :].astype(BF16)
        vt_ref[0, hd, HEAD_DIM:, :] = ones_row

    qm_ref[0] = (_dot(a, w_qm_ref[...]) * SCALE).astype(BF16)


def _proj_b(h3, g_sh, g_pre, w_k, w_vt, w_f, b_f, w_qt, w_qm, tm):
    b, s, d = h3.shape
    gsel = _gate_selector()
    return pl.pallas_call(
        _proj_b_kernel,
        grid=(b, s // tm),
        in_specs=[pl.BlockSpec((1, tm, d), lambda i, j: (i, j, 0)),
                  _full(g_sh.shape), _full(g_pre.shape), _full(w_k.shape), _full(w_vt.shape),
                  _full(w_f.shape), _full(b_f.shape), _full(w_qt.shape), _full(w_qm.shape),
                  _full(gsel.shape)],
        out_specs=[
            pl.BlockSpec((1, FOX_HEADS, tm, LANES), lambda i, j: (i, 0, j, 0)),
            pl.BlockSpec((1, FOX_HEADS * LANES, tm), lambda i, j: (i, 0, j)),
            pl.BlockSpec((1, FOX_HEADS, V_ROWS, tm), lambda i, j: (i, 0, 0, j)),
            pl.BlockSpec((1, tm, MEM_WIDTH), lambda i, j: (i, j, 0)),
        ],
        out_shape=[
            jax.ShapeDtypeStruct((b, FOX_HEADS, s, LANES), BF16),
            jax.ShapeDtypeStruct((b, FOX_HEADS * LANES, s), BF16),
            jax.ShapeDtypeStruct((b, FOX_HEADS, V_ROWS, s), BF16),
            jax.ShapeDtypeStruct((b, s, MEM_WIDTH), BF16),
        ],
        scratch_shapes=[pltpu.VMEM((1, LANES), F32)],
        compiler_params=_params("arbitrary", "arbitrary"),
        name="proj_b",
    )(h3, g_sh, g_pre, w_k, w_vt, w_f, b_f, w_qt, w_qm, gsel)


def _fox_kernel(k_ref, qt_ref, vt_ref, o_ref, s0_ref, s1_ref, bm0_ref, bm1_ref,
                p0_ref, p1_ref, a0_ref, a1_ref, bias_ref, m_ref, acc_ref):
    tk = FOX_TK
    tq = FOX_TQ
    n_q = k_ref.shape[2] // tq
    assert tq == 2 * tk and n_q >= 2 and n_q % 2 == 0
    sub = 2
    s_refs = (s0_ref, s1_ref)
    bm_refs = (bm0_ref, bm1_ref)
    p_refs = (p0_ref, p1_ref)
    a_refs = (a0_ref, a1_ref)

    def issue_scores(par, qi, kb, with_max):
        k0 = pl.multiple_of(kb * tk, tk)
        q0 = pl.multiple_of(qi * tq, tq)
        for hh in range(2):
            s = _dot(k_ref[0, hh, pl.ds(k0, tk), :],
                     qt_ref[0, hh * LANES:(hh + 1) * LANES, pl.ds(q0, tq)])
            s_refs[par][hh] = s
            if with_max:
                bm_refs[par][hh] = jnp.max(s, axis=0, keepdims=True)

    def accumulate(par, qi, kb):
        k0 = pl.multiple_of(kb * tk, tk)
        q0 = pl.multiple_of(qi * tq, tq)
        for hh in range(2):
            pv = _dot(vt_ref[0, hh, :, pl.ds(k0, tk)], p_refs[par][hh])
            acc_ref[hh, :, pl.ds(q0, tq)] = a_refs[par][hh] * acc_ref[hh, :, pl.ds(q0, tq)] + pv

    def softmax(idx, qi, mask_id):
        q0 = pl.multiple_of(qi * tq, tq)
        par = idx % 2
        for hh in range(2):
            s = s_refs[par][hh]
            if mask_id is None:
                m_blk = bm_refs[par][hh]
            else:
                s = s + bias_ref[mask_id]
                m_blk = jnp.max(s, axis=0, keepdims=True)
            m_old = m_ref[hh, :, pl.ds(q0, tq)]
            m_new = jnp.maximum(m_old, m_blk)
            m_ref[hh, :, pl.ds(q0, tq)] = m_new
            a_refs[par][hh] = jnp.exp2(m_old - m_new)
            p_refs[par][hh] = jnp.exp2(s - m_new).astype(BF16)

    def finalize(qi):
        q0 = pl.multiple_of(qi * tq, tq)
        halves = [acc_ref[hh, :HEAD_DIM, pl.ds(q0, tq)]
                  / acc_ref[hh, HEAD_DIM:HEAD_DIM + 1, pl.ds(q0, tq)] for hh in range(2)]
        o_ref[0, pl.ds(q0, tq), :] = jnp.concatenate(halves, axis=0).T.astype(BF16)

    def step(idx, prev, cur, nxt, mask_id, after_accumulate=None):
        accumulate((idx + 1) % 2, *prev)
        if after_accumulate is not None:
            after_accumulate()
        issue_scores((idx + 1) % 2, *nxt, with_max=mask_id is None)
        softmax(idx, cur[0], mask_id)

    key_pos = lax.broadcasted_iota(jnp.int32, (tk, tq), 0)
    qry_pos = lax.broadcasted_iota(jnp.int32, (tk, tq), 1)
    for d in range(sub):
        bias_ref[d] = jnp.where(key_pos + d * tk <= qry_pos, 0.0, -jnp.inf)
    m_ref[...] = jnp.full(m_ref.shape, -jnp.inf, F32)
    acc_ref[...] = jnp.zeros(acc_ref.shape, F32)
    p_refs[1][...] = jnp.zeros(p_refs[1].shape, BF16)
    a_refs[1][...] = jnp.zeros(a_refs[1].shape, F32)

    def below_next(qi, kb):
        more = kb + 1 < qi * sub
        wrap = qi + 1 < n_q
        return (jnp.where(more, qi, jnp.where(wrap, qi + 1, 0)),
                jnp.where(more, kb + 1, 0))

    issue_scores(0, 1, 0, with_max=True)

    def below_body(_, carry):
        prev, cur = carry[:2], carry[2:]
        for idx in range(STEPS_PER_ITER):
            nxt = below_next(*cur)
            step(idx, prev, cur, nxt, None)
            prev, cur = cur, nxt
        return prev + cur

    n_below = sub * n_q * (n_q - 1) // 2
    assert STEPS_PER_ITER == 2 * sub and n_below % STEPS_PER_ITER == 0
    zero = jnp.int32(0)
    carry = lax.fori_loop(0, n_below // STEPS_PER_ITER, below_body,
                          (zero, zero, jnp.int32(1), zero))

    def diag_body(j, prev):
        qa = 2 * j
        qb = qa + 1
        ka = qa * sub
        kb = qb * sub
        step(0, prev, (qa, ka), (qa, ka + 1), 0)
        step(1, (qa, ka), (qa, ka + 1), (qb, kb), 1)
        step(2, (qa, ka + 1), (qb, kb), (qb, kb + 1), 0, after_accumulate=lambda: finalize(qa))
        qn = jnp.minimum(qb + 1, n_q - 1)
        step(3, (qb, kb), (qb, kb + 1), (qn, qn * sub), 1)

        @pl.when(j >= 1)
        def _():
            finalize(qa - 1)

        return (qb, kb + 1)

    prev = lax.fori_loop(0, n_q // 2, diag_body, carry[:2])
    accumulate(1, *prev)
    finalize(n_q - 1)


def _fox_attention(kp, qt, vt):
    b, _, s, _ = kp.shape
    pairs = FOX_HEADS // 2
    return pl.pallas_call(
        _fox_kernel,
        grid=(b, pairs),
        in_specs=[
            pl.BlockSpec((1, 2, s, LANES), lambda i, p: (i, p, 0, 0)),
            pl.BlockSpec((1, 2 * LANES, s), lambda i, p: (i, p, 0)),
            pl.BlockSpec((1, 2, V_ROWS, s), lambda i, p: (i, p, 0, 0)),
        ],
        out_specs=pl.BlockSpec((1, s, LANES), lambda i, p: (i, 0, p)),
        out_shape=jax.ShapeDtypeStruct((b, s, MAIN_WIDTH), BF16),
        scratch_shapes=[
            pltpu.VMEM((2, FOX_TK, FOX_TQ), F32),
            pltpu.VMEM((2, FOX_TK, FOX_TQ), F32),
            pltpu.VMEM((2, 1, FOX_TQ), F32),
            pltpu.VMEM((2, 1, FOX_TQ), F32),
            pltpu.VMEM((2, FOX_TK, FOX_TQ), BF16),
            pltpu.VMEM((2, FOX_TK, FOX_TQ), BF16),
            pltpu.VMEM((2, 1, FOX_TQ), F32),
            pltpu.VMEM((2, 1, FOX_TQ), F32),
            pltpu.VMEM((2, FOX_TK, FOX_TQ), F32),
            pltpu.VMEM((2, 1, s), F32),
            pltpu.VMEM((2, V_ROWS, s), F32),
        ],
        compiler_params=_params("arbitrary", "arbitrary"),
        name="fox_attention",
    )(kp, qt, vt)


def _mixer_b_kernel(h_ref, attn_ref, qm_ref, kt_ref, vm_ref, w_out_ref, g_post_ref, o_ref):
    h = h_ref[...]
    mem_o = _memory_attention(qm_ref[...], kt_ref[0, 0], vm_ref[0, 0]).astype(BF16)
    y = _dot(attn_ref[...], w_out_ref[0, :MAIN_WIDTH, :]) + _dot(mem_o, w_out_ref[0, MAIN_WIDTH:, :])
    o_ref[...] = h + _rms(y, g_post_ref[...])


def _mixer_b(h, attn, qm, kt, vm, layer, w_out, g_post, seq, tm):
    t, d = h.shape
    per_batch = seq // tm
    tile = lambda w: pl.BlockSpec((tm, w), lambda i: (i, 0))
    return pl.pallas_call(
        _mixer_b_kernel,
        grid=(t // tm,),
        in_specs=[tile(d), tile(MAIN_WIDTH), tile(MEM_WIDTH),
                  pl.BlockSpec((1, 1) + kt.shape[2:], lambda i: (layer, i // per_batch, 0, 0)),
                  pl.BlockSpec((1, 1) + vm.shape[2:], lambda i: (layer, i // per_batch, 0, 0)),
                  _layer(w_out.shape, layer), _full(g_post.shape)],
        out_specs=tile(d),
        out_shape=jax.ShapeDtypeStruct((t, d), F32),
        compiler_params=_params("arbitrary"),
        name="mixer_b",
    )(h, attn, qm, kt, vm, w_out, g_post)


def kernel(x, mem, ln_mix_pre, ln_mix_post, ln_ffn_pre, ln_ffn_post, ln_mem, w_mem_kv, w_out,
           w_ffn_gate, w_ffn_up, w_ffn_down, w_in_a, w_spatial, b_spatial, ln_v_g, ln_v_b,
           ln_shared, w_shared_kv, b_forget, w_in_b):
    b, s, d = x.shape
    t = b * s
    row = lambda g: g.reshape(1, -1)
    bf = lambda w: w.astype(BF16)

    w_out_bf, wg_bf, wu_bf, wd_bf = bf(w_out), bf(w_ffn_gate), bf(w_ffn_up), bf(w_ffn_down)
    kt_mem, v_mem = _memkv(mem, ln_mem, bf(w_mem_kv))
    h = x.reshape(t, d)

    h = _mixer_a(h, kt_mem, v_mem, 0, row(ln_mix_pre[0]), bf(w_in_a[0]), w_spatial[0],
                 b_spatial[0].reshape(A_GROUPS, CHUNK, 1), row(ln_v_g[0]), row(ln_v_b[0]),
                 w_out_bf, row(ln_mix_post[0]), s, TOKEN_TILE)
    h = _ffn(h, 0, row(ln_ffn_pre[0]), wg_bf, wu_bf, wd_bf, row(ln_ffn_post[0]), TOKEN_TILE)

    pad = LANES - FOX_HEADS
    w_f = jnp.pad(w_shared_kv[:, 2 * MAIN_WIDTH:], ((0, 0), (0, pad)))
    b_f = jnp.pad(b_forget, (0, pad)).reshape(1, LANES)
    kp, qt, vt, qm = _proj_b(
        h.reshape(b, s, d), row(ln_shared), row(ln_mix_pre[1]),
        bf(w_shared_kv[:, :MAIN_WIDTH]), bf(w_shared_kv[:, MAIN_WIDTH:2 * MAIN_WIDTH].T),
        bf(w_f), b_f, bf(w_in_b[0][:, :MAIN_WIDTH].T), bf(w_in_b[0][:, MAIN_WIDTH:]), TOKEN_TILE)
    attn = _fox_attention(kp, qt, vt)
    h = _mixer_b(h, attn.reshape(t, MAIN_WIDTH), qm.reshape(t, MEM_WIDTH), kt_mem, v_mem, 1,
                 w_out_bf, row(ln_mix_post[1]), s, TOKEN_TILE)
    h = _ffn(h, 1, row(ln_ffn_pre[1]), wg_bf, wu_bf, wd_bf, row(ln_ffn_post[1]), TOKEN_TILE)
    return h.reshape(b, s, d)
```

```python
import math

import numpy as np
import jax
import jax.numpy as jnp
from jax import lax
from jax.experimental import pallas as pl
from jax.experimental.pallas import tpu as pltpu

F32 = jnp.float32
BF16 = jnp.bfloat16

D_MODEL = 1024
HEAD_DIM = 64
MEM_HEADS = 4
MEM_WIDTH = MEM_HEADS * HEAD_DIM
MAIN_WIDTH = D_MODEL - MEM_WIDTH
CHUNK = 128
A_GROUPS = 6
FOX_HEADS = MAIN_WIDTH // HEAD_DIM
RMS_EPS = 1e-6
LN_EPS = 1e-5
SCALE = HEAD_DIM ** -0.5
LOG2E = math.log2(math.e)
LANES = 128
VMEM_LIMIT = 56 * 1024 * 1024
TOKEN_TILE = 512
MIXER_A_TILE = 1024
MIXER_SUB_TILES = 4
PROJ_TILE = 1024
PROJ_SUB_TILES = 4
FFN_TILE = 1024
FFN_SUB_ROWS = 256

N_PIECES = 3
PIECE_STRIDE = 16
CT_ROW = 36
V_ROWS = 80
FOX_TK = 256
FOX_TQ = 512
STEPS_PER_ITER = 4

NT_DIMS = (((1,), (1,)), ((), ()))


def _dot(a, b):
    return jnp.dot(a, b, preferred_element_type=F32)


def _dot_nt(a, b):
    return lax.dot_general(a, b, NT_DIMS, preferred_element_type=F32)


def _rms(x, g):
    return (x * lax.rsqrt(jnp.mean(x * x, axis=-1, keepdims=True) + RMS_EPS)) * g


def _split_bf16(x):
    p1 = x.astype(BF16)
    r1 = x - p1.astype(F32)
    p2 = r1.astype(BF16)
    p3 = (r1 - p2.astype(F32)).astype(BF16)
    return p1, p2, p3


def _params(*sem):
    return pltpu.CompilerParams(dimension_semantics=sem, vmem_limit_bytes=VMEM_LIMIT)


def _full(shape):
    n = len(shape)
    return pl.BlockSpec(shape, lambda *_: (0,) * n)


def _layer(shape, layer):
    n = len(shape) - 1
    return pl.BlockSpec((1,) + tuple(shape[1:]), lambda *_: (layer,) + (0,) * n)


def _memkv_kernel(mem_ref, g_ref, w_ref, kt_ref, v_ref):
    m = mem_ref.shape[1]
    n = _rms(mem_ref[0], g_ref[0])
    kv = _dot(n.astype(BF16), w_ref[0])
    k_t = kv[:, :MEM_WIDTH].T
    v = kv[:, MEM_WIDTH:]
    head_of_row = lax.broadcasted_iota(jnp.int32, (MEM_WIDTH, 1), 0) // HEAD_DIM
    head_of_lane = lax.broadcasted_iota(jnp.int32, (1, MEM_WIDTH), 1) // HEAD_DIM
    for hh in range(MEM_HEADS):
        kt_ref[0, 0, :, hh * m:(hh + 1) * m] = jnp.where(head_of_row == hh, k_t, 0.0).astype(BF16)
        v_ref[0, 0, hh * m:(hh + 1) * m, :] = jnp.where(head_of_lane == hh, v, 0.0).astype(BF16)


def _memkv(mem, ln_mem, w_mem_kv):
    depth = ln_mem.shape[0]
    b, m, d = mem.shape
    n_out = w_mem_kv.shape[-1]
    return pl.pallas_call(
        _memkv_kernel,
        grid=(depth, b),
        in_specs=[
            pl.BlockSpec((1, m, d), lambda l, i: (i, 0, 0)),
            pl.BlockSpec((1, 1, d), lambda l, i: (l, 0, 0)),
            pl.BlockSpec((1, d, n_out), lambda l, i: (l, 0, 0)),
        ],
        out_specs=[
            pl.BlockSpec((1, 1, MEM_WIDTH, MEM_HEADS * m), lambda l, i: (l, i, 0, 0)),
            pl.BlockSpec((1, 1, MEM_HEADS * m, MEM_WIDTH), lambda l, i: (l, i, 0, 0)),
        ],
        out_shape=[
            jax.ShapeDtypeStruct((depth, b, MEM_WIDTH, MEM_HEADS * m), BF16),
            jax.ShapeDtypeStruct((depth, b, MEM_HEADS * m, MEM_WIDTH), BF16),
        ],
        compiler_params=_params("arbitrary", "arbitrary"),
        name="memkv",
    )(mem, ln_mem.reshape(depth, 1, d), w_mem_kv)


def _memory_scores(qm, kt):
    return _dot(qm, kt)


def _memory_readout(s_all, v):
    m = s_all.shape[1] // MEM_HEADS
    probs = []
    for hh in range(MEM_HEADS):
        s = s_all[:, hh * m:(hh + 1) * m]
        p = jnp.exp(s - jnp.max(s, axis=-1, keepdims=True))
        inv = 1.0 / jnp.sum(p, axis=-1, keepdims=True)
        probs.append((p * inv).astype(BF16))
    return _dot(jnp.concatenate(probs, axis=1), v)


def _mixer_a_kernel(h_ref, kt_ref, vm_ref, g_pre_ref, w_in_ref, wsp_ref, bsp_ref, lng_ref, lnb_ref,
                    w_out_ref, g_post_ref, o_ref, mixed_ref):
    sub = h_ref.shape[0] // MIXER_SUB_TILES
    n_chunks = sub // CHUNK
    row = lax.broadcasted_iota(jnp.int32, (CHUNK, CHUNK), 0)
    col = lax.broadcasted_iota(jnp.int32, (CHUNK, CHUNK), 1)
    causal = row >= col
    w_sp = [jnp.where(causal, wsp_ref[g], 0.0).astype(BF16) for g in range(A_GROUPS)]

    def in_proj(r):
        h = h_ref[r * sub:(r + 1) * sub, :]
        a = _rms(h, g_pre_ref[...]).astype(BF16)
        return h, _dot(a, w_in_ref[...])

    def mix(r, h, proj):
        base = r * sub
        s_mem = _memory_scores((proj[:, 2 * MAIN_WIDTH:] * SCALE).astype(BF16), kt_ref[0, 0])
        u = jax.nn.gelu(proj[:, :MAIN_WIDTH])
        v = jax.nn.gelu(proj[:, MAIN_WIDTH:2 * MAIN_WIDTH])
        mu = jnp.mean(v, axis=-1, keepdims=True)
        vc = v - mu
        vn = vc * lax.rsqrt(jnp.mean(vc * vc, axis=-1, keepdims=True) + LN_EPS)
        vn = (vn * lng_ref[...] + lnb_ref[...]).astype(BF16)
        for g in range(A_GROUPS):
            bias = bsp_ref[g]
            cols = slice(g * CHUNK, (g + 1) * CHUNK)
            v_g = jnp.concatenate([vn[c * CHUNK:(c + 1) * CHUNK, cols] for c in range(n_chunks)],
                                  axis=1)
            s_g = _dot(w_sp[g], v_g)
            for c in range(n_chunks):
                rows = slice(c * CHUNK, (c + 1) * CHUNK)
                s = s_g[:, c * CHUNK:(c + 1) * CHUNK] + bias
                mixed_ref[base + c * CHUNK:base + (c + 1) * CHUNK, cols] = (
                    u[rows, cols] * s).astype(BF16)
        mixed_ref[base:base + sub, MAIN_WIDTH:] = _memory_readout(s_mem, vm_ref[0, 0]).astype(BF16)
        y = _dot(mixed_ref[base:base + sub, :], w_out_ref[0])
        o_ref[base:base + sub, :] = h + _rms(y, g_post_ref[...])

    nxt = in_proj(0)
    for r in range(MIXER_SUB_TILES):
        cur = nxt
        if r + 1 < MIXER_SUB_TILES:
            nxt = in_proj(r + 1)
        mix(r, *cur)


def _mixer_a(h, kt, vm, layer, g_pre, w_in, wsp, bsp, lng, lnb, w_out, g_post, seq, tm):
    t, d = h.shape
    per_batch = seq // tm
    return pl.pallas_call(
        _mixer_a_kernel,
        grid=(t // tm,),
        in_specs=[
            pl.BlockSpec((tm, d), lambda i: (i, 0)),
            pl.BlockSpec((1, 1) + kt.shape[2:], lambda i: (layer, i // per_batch, 0, 0)),
            pl.BlockSpec((1, 1) + vm.shape[2:], lambda i: (layer, i // per_batch, 0, 0)),
            _full(g_pre.shape), _full(w_in.shape), _full(wsp.shape), _full(bsp.shape),
            _full(lng.shape), _full(lnb.shape), _layer(w_out.shape, layer), _full(g_post.shape),
        ],
        out_specs=pl.BlockSpec((tm, d), lambda i: (i, 0)),
        out_shape=jax.ShapeDtypeStruct((t, d), F32),
        scratch_shapes=[pltpu.VMEM((tm, d), BF16)],
        compiler_params=_params("arbitrary"),
        name="mixer_a",
    )(h, kt, vm, g_pre, w_in, wsp, bsp, lng, lnb, w_out, g_post)


def _ffn_kernel(h_ref, g_pre_ref, wg_ref, wu_ref, wd_ref, g_post_ref, o_ref):
    sub = FFN_SUB_ROWS
    for r in range(h_ref.shape[0] // sub):
        rows = slice(r * sub, (r + 1) * sub)
        h = h_ref[rows, :]
        f = _rms(h, g_pre_ref[...]).astype(BF16)
        gate = _dot(f, wg_ref[0])
        up = _dot(f, wu_ref[0])
        act = (jax.nn.silu(gate) * up).astype(BF16)
        y = _dot(act, wd_ref[0])
        o_ref[rows, :] = h + _rms(y, g_post_ref[...])


def _ffn(h, layer, g_pre, wg, wu, wd, g_post, tm):
    t, d = h.shape
    return pl.pallas_call(
        _ffn_kernel,
        grid=(t // tm,),
        in_specs=[
            pl.BlockSpec((tm, d), lambda i: (i, 0)),
            _full(g_pre.shape), _layer(wg.shape, layer), _layer(wu.shape, layer),
            _layer(wd.shape, layer), _full(g_post.shape),
        ],
        out_specs=pl.BlockSpec((tm, d), lambda i: (i, 0)),
        out_shape=jax.ShapeDtypeStruct((t, d), F32),
        compiler_params=_params("arbitrary"),
        name="ffn",
    )(h, g_pre, wg, wu, wd, g_post)


def _gate_selector():
    g = np.zeros((LANES, LANES), np.float32)
    for i in range(N_PIECES):
        for hd in range(FOX_HEADS):
            g[PIECE_STRIDE * i + hd, HEAD_DIM + N_PIECES * hd + i] = -1.0
    return jnp.asarray(g, dtype=BF16)


def _pack_pieces(pieces):
    packed = pieces[0].astype(F32)
    for i in range(1, N_PIECES):
        packed = packed + pltpu.roll(pieces[i].astype(F32), PIECE_STRIDE * i, axis=1)
    return packed.astype(BF16)


def _proj_b_kernel(h_ref, g_sh_ref, g_pre_ref, w_k_ref, w_vt_ref, w_f_ref, b_f_ref, w_qt_ref,
                   w_qm_ref, gsel_ref, kp_ref, qt_ref, vt_ref, qm_ref, carry_ref):
    sub = h_ref.shape[1] // PROJ_SUB_TILES

    @pl.when(pl.program_id(1) == 0)
    def _():
        carry_ref[...] = jnp.zeros_like(carry_ref)

    lane = lax.broadcasted_iota(jnp.int32, (1, LANES), 1)
    row = lax.broadcasted_iota(jnp.int32, (sub, sub), 0)
    col = lax.broadcasted_iota(jnp.int32, (sub, sub), 1)
    tri = (row >= col).astype(BF16)
    low = lane < HEAD_DIM
    ones_lanes = jnp.where((lane >= HEAD_DIM + CT_ROW) & (lane < HEAD_DIM + CT_ROW + N_PIECES),
                           1.0, 0.0)
    rid = lax.broadcasted_iota(jnp.int32, (HEAD_DIM, 1), 0)
    rid_v = lax.broadcasted_iota(jnp.int32, (V_ROWS - HEAD_DIM, sub), 0)
    ones_row = jnp.where(rid_v == 0, 1.0, 0.0).astype(BF16)

    def project(r):
        rows = slice(r * sub, (r + 1) * sub)
        h = h_ref[0, rows, :]
        hn = h * lax.rsqrt(jnp.mean(h * h, axis=-1, keepdims=True) + RMS_EPS)
        s_in = (hn * g_sh_ref[...]).astype(BF16)
        a = (hn * g_pre_ref[...]).astype(BF16)
        log_f = jax.nn.log_sigmoid(_dot(s_in, w_f_ref[...]) + b_f_ref[...])
        log_f = jnp.where(lane < FOX_HEADS, log_f, 0.0)
        k = _dot(s_in, w_k_ref[...])
        qt = _dot_nt(w_qt_ref[...], a) * (SCALE * LOG2E)
        vt = _dot_nt(w_vt_ref[...], s_in)
        qm_ref[0, rows, :] = (_dot(a, w_qm_ref[...]) * SCALE).astype(BF16)
        return log_f, k, qt, vt

    def assemble(r, log_f, k, qt, vt):
        rows = slice(r * sub, (r + 1) * sub)
        sums = _dot(tri, _pack_pieces(_split_bf16(log_f)))
        c = sums + carry_ref[...]
        for i in range(1, N_PIECES):
            c = c + pltpu.roll(sums, LANES - PIECE_STRIDE * i, axis=1)
        c = jnp.where(lane < FOX_HEADS, c, 0.0)
        carry_ref[...] = c[sub - 1:sub, :]
        c2 = c * LOG2E

        aug = _dot(_pack_pieces(_split_bf16(c2)), gsel_ref[...]) + ones_lanes
        for p in range(FOX_HEADS // 2):
            blk = k[:, p * LANES:(p + 1) * LANES]
            kp_ref[0, 2 * p, rows, :] = jnp.where(low, blk, aug).astype(BF16)
            kp_ref[0, 2 * p + 1, rows, :] = jnp.where(
                low, pltpu.roll(blk, HEAD_DIM, axis=1), aug).astype(BF16)

        t1, t2, t3 = _split_bf16(c2.T)
        for hd in range(FOX_HEADS):
            sel = jnp.where((rid >= N_PIECES * hd) & (rid < N_PIECES * (hd + 1)), 1.0, 0.0)
            aug_q = jnp.where(rid == CT_ROW, t1[hd:hd + 1, :].astype(F32),
                              jnp.where(rid == CT_ROW + 1, t2[hd:hd + 1, :].astype(F32),
                                        jnp.where(rid == CT_ROW + 2, t3[hd:hd + 1, :].astype(F32),
                                                  sel)))
            qt_ref[0, hd * LANES:hd * LANES + HEAD_DIM, rows] = (
                qt[hd * HEAD_DIM:(hd + 1) * HEAD_DIM, :].astype(BF16))
            qt_ref[0, hd * LANES + HEAD_DIM:(hd + 1) * LANES, rows] = aug_q.astype(BF16)
            vt_ref[0, hd, :HEAD_DIM, rows] = vt[hd * HEAD_DIM:(hd + 1) * HEAD_DIM, :].astype(BF16)
            vt_ref[0, hd, HEAD_DIM:, rows] = ones_row

    nxt = project(0)
    for r in range(PROJ_SUB_TILES):
        cur = nxt
        if r + 1 < PROJ_SUB_TILES:
            nxt = project(r + 1)
        assemble(r, *cur)


def _proj_b(h3, g_sh, g_pre, w_k, w_vt, w_f, b_f, w_qt, w_qm, tm):
    b, s, d = h3.shape
    gsel = _gate_selector()
    return pl.pallas_call(
        _proj_b_kernel,
        grid=(b, s // tm),
        in_specs=[pl.BlockSpec((1, tm, d), lambda i, j: (i, j, 0)),
                  _full(g_sh.shape), _full(g_pre.shape), _full(w_k.shape), _full(w_vt.shape),
                  _full(w_f.shape), _full(b_f.shape), _full(w_qt.shape), _full(w_qm.shape),
                  _full(gsel.shape)],
        out_specs=[
            pl.BlockSpec((1, FOX_HEADS, tm, LANES), lambda i, j: (i, 0, j, 0)),
            pl.BlockSpec((1, FOX_HEADS * LANES, tm), lambda i, j: (i, 0, j)),
            pl.BlockSpec((1, FOX_HEADS, V_ROWS, tm), lambda i, j: (i, 0, 0, j)),
            pl.BlockSpec((1, tm, MEM_WIDTH), lambda i, j: (i, j, 0)),
        ],
        out_shape=[
            jax.ShapeDtypeStruct((b, FOX_HEADS, s, LANES), BF16),
            jax.ShapeDtypeStruct((b, FOX_HEADS * LANES, s), BF16),
            jax.ShapeDtypeStruct((b, FOX_HEADS, V_ROWS, s), BF16),
            jax.ShapeDtypeStruct((b, s, MEM_WIDTH), BF16),
        ],
        scratch_shapes=[pltpu.VMEM((1, LANES), F32)],
        compiler_params=_params("arbitrary", "arbitrary"),
        name="proj_b",
    )(h3, g_sh, g_pre, w_k, w_vt, w_f, b_f, w_qt, w_qm, gsel)


def _fox_kernel(k_ref, qt_ref, vt_ref, o_ref, s0_ref, s1_ref, bm0_ref, bm1_ref,
                p0_ref, p1_ref, a0_ref, a1_ref, bias_ref, m_ref, acc_ref):
    tk = FOX_TK
    tq = FOX_TQ
    n_q = k_ref.shape[2] // tq
    assert tq == 2 * tk and n_q >= 2 and n_q % 2 == 0
    sub = 2
    s_refs = (s0_ref, s1_ref)
    bm_refs = (bm0_ref, bm1_ref)
    p_refs = (p0_ref, p1_ref)
    a_refs = (a0_ref, a1_ref)

    def issue_scores(par, qi, kb, with_max):
        k0 = pl.multiple_of(kb * tk, tk)
        q0 = pl.multiple_of(qi * tq, tq)
        for hh in range(2):
            s = _dot(k_ref[0, hh, pl.ds(k0, tk), :],
                     qt_ref[0, hh * LANES:(hh + 1) * LANES, pl.ds(q0, tq)])
            s_refs[par][hh] = s
            if with_max:
                bm_refs[par][hh] = jnp.max(s, axis=0, keepdims=True)

    def accumulate(par, qi, kb):
        k0 = pl.multiple_of(kb * tk, tk)
        q0 = pl.multiple_of(qi * tq, tq)
        for hh in range(2):
            pv = _dot(vt_ref[0, hh, :, pl.ds(k0, tk)], p_refs[par][hh])
            acc_ref[hh, :, pl.ds(q0, tq)] = a_refs[par][hh] * acc_ref[hh, :, pl.ds(q0, tq)] + pv

    def softmax(idx, qi, mask_id):
        q0 = pl.multiple_of(qi * tq, tq)
        par = idx % 2
        for hh in range(2):
            s = s_refs[par][hh]
            if mask_id is None:
                m_blk = bm_refs[par][hh]
            else:
                s = s + bias_ref[mask_id]
                m_blk = jnp.max(s, axis=0, keepdims=True)
            m_old = m_ref[hh, :, pl.ds(q0, tq)]
            m_new = jnp.maximum(m_old, m_blk)
            m_ref[hh, :, pl.ds(q0, tq)] = m_new
            a_refs[par][hh] = jnp.exp2(m_old - m_new)
            p_refs[par][hh] = jnp.exp2(s - m_new).astype(BF16)

    def finalize(qi):
        q0 = pl.multiple_of(qi * tq, tq)
        halves = [acc_ref[hh, :HEAD_DIM, pl.ds(q0, tq)]
                  / acc_ref[hh, HEAD_DIM:HEAD_DIM + 1, pl.ds(q0, tq)] for hh in range(2)]
        o_ref[0, pl.ds(q0, tq), :] = jnp.concatenate(halves, axis=0).T.astype(BF16)

    def step(idx, prev, cur, nxt, mask_id, after_accumulate=None):
        accumulate((idx + 1) % 2, *prev)
        if after_accumulate is not None:
            after_accumulate()
        issue_scores((idx + 1) % 2, *nxt, with_max=mask_id is None)
        softmax(idx, cur[0], mask_id)

    key_pos = lax.broadcasted_iota(jnp.int32, (tk, tq), 0)
    qry_pos = lax.broadcasted_iota(jnp.int32, (tk, tq), 1)
    for d in range(sub):
        bias_ref[d] = jnp.where(key_pos + d * tk <= qry_pos, 0.0, -jnp.inf)
    m_ref[...] = jnp.full(m_ref.shape, -jnp.inf, F32)
    acc_ref[...] = jnp.zeros(acc_ref.shape, F32)
    p_refs[1][...] = jnp.zeros(p_refs[1].shape, BF16)
    a_refs[1][...] = jnp.zeros(a_refs[1].shape, F32)

    def below_next(qi, kb):
        more = kb + 1 < qi * sub
        wrap = qi + 1 < n_q
        return (jnp.where(more, qi, jnp.where(wrap, qi + 1, 0)),
                jnp.where(more, kb + 1, 0))

    issue_scores(0, 1, 0, with_max=True)

    def below_body(_, carry):
        prev, cur = carry[:2], carry[2:]
        for idx in range(STEPS_PER_ITER):
            nxt = below_next(*cur)
            step(idx, prev, cur, nxt, None)
            prev, cur = cur, nxt
        return prev + cur

    n_below = sub * n_q * (n_q - 1) // 2
    assert STEPS_PER_ITER == 2 * sub and n_below % STEPS_PER_ITER == 0
    zero = jnp.int32(0)
    carry = lax.fori_loop(0, n_below // STEPS_PER_ITER, below_body,
                          (zero, zero, jnp.int32(1), zero))

    def diag_body(j, prev):
        qa = 2 * j
        qb = qa + 1
        ka = qa * sub
        kb = qb * sub
        step(0, prev, (qa, ka), (qa, ka + 1), 0)
        step(1, (qa, ka), (qa, ka + 1), (qb, kb), 1)
        step(2, (qa, ka + 1), (qb, kb), (qb, kb + 1), 0, after_accumulate=lambda: finalize(qa))
        qn = jnp.minimum(qb + 1, n_q - 1)
        step(3, (qb, kb), (qb, kb + 1), (qn, qn * sub), 1)

        @pl.when(j >= 1)
        def _():
            finalize(qa - 1)

        return (qb, kb + 1)

    prev = lax.fori_loop(0, n_q // 2, diag_body, carry[:2])
    accumulate(1, *prev)
    finalize(n_q - 1)


def _fox_attention(kp, qt, vt):
    b, _, s, _ = kp.shape
    pairs = FOX_HEADS // 2
    return pl.pallas_call(
        _fox_kernel,
        grid=(b, pairs),
        in_specs=[
            pl.BlockSpec((1, 2, s, LANES), lambda i, p: (i, p, 0, 0)),
            pl.BlockSpec((1, 2 * LANES, s), lambda i, p: (i, p, 0)),
            pl.BlockSpec((1, 2, V_ROWS, s), lambda i, p: (i, p, 0, 0)),
        ],
        out_specs=pl.BlockSpec((1, s, LANES), lambda i, p: (i, 0, p)),
        out_shape=jax.ShapeDtypeStruct((b, s, MAIN_WIDTH), BF16),
        scratch_shapes=[
            pltpu.VMEM((2, FOX_TK, FOX_TQ), F32),
            pltpu.VMEM((2, FOX_TK, FOX_TQ), F32),
            pltpu.VMEM((2, 1, FOX_TQ), F32),
            pltpu.VMEM((2, 1, FOX_TQ), F32),
            pltpu.VMEM((2, FOX_TK, FOX_TQ), BF16),
            pltpu.VMEM((2, FOX_TK, FOX_TQ), BF16),
            pltpu.VMEM((2, 1, FOX_TQ), F32),
            pltpu.VMEM((2, 1, FOX_TQ), F32),
            pltpu.VMEM((2, FOX_TK, FOX_TQ), F32),
            pltpu.VMEM((2, 1, s), F32),
            pltpu.VMEM((2, V_ROWS, s), F32),
        ],
        compiler_params=_params("arbitrary", "arbitrary"),
        name="fox_attention",
    )(kp, qt, vt)


def _mixer_b_kernel(h_ref, attn_ref, qm_ref, kt_ref, vm_ref, w_out_ref, g_post_ref, o_ref):
    h = h_ref[...]
    s_mem = _memory_scores(qm_ref[...], kt_ref[0, 0])
    y_main = _dot(attn_ref[...], w_out_ref[0, :MAIN_WIDTH, :])
    mem_o = _memory_readout(s_mem, vm_ref[0, 0]).astype(BF16)
    y = y_main + _dot(mem_o, w_out_ref[0, MAIN_WIDTH:, :])
    o_ref[...] = h + _rms(y, g_post_ref[...])


def _mixer_b(h, attn, qm, kt, vm, layer, w_out, g_post, seq, tm):
    t, d = h.shape
    per_batch = seq // tm
    tile = lambda w: pl.BlockSpec((tm, w), lambda i: (i, 0))
    return pl.pallas_call(
        _mixer_b_kernel,
        grid=(t // tm,),
        in_specs=[tile(d), tile(MAIN_WIDTH), tile(MEM_WIDTH),
                  pl.BlockSpec((1, 1) + kt.shape[2:], lambda i: (layer, i // per_batch, 0, 0)),
                  pl.BlockSpec((1, 1) + vm.shape[2:], lambda i: (layer, i // per_batch, 0, 0)),
                  _layer(w_out.shape, layer), _full(g_post.shape)],
        out_specs=tile(d),
        out_shape=jax.ShapeDtypeStruct((t, d), F32),
        compiler_params=_params("arbitrary"),
        name="mixer_b",
    )(h, attn, qm, kt, vm, w_out, g_post)


def kernel(x, mem, ln_mix_pre, ln_mix_post, ln_ffn_pre, ln_ffn_post, ln_mem, w_mem_kv, w_out,
           w_ffn_gate, w_ffn_up, w_ffn_down, w_in_a, w_spatial, b_spatial, ln_v_g, ln_v_b,
           ln_shared, w_shared_kv, b_forget, w_in_b):
    b, s, d = x.shape
    t = b * s
    row = lambda g: g.reshape(1, -1)
    bf = lambda w: w.astype(BF16)

    w_out_bf, wg_bf, wu_bf, wd_bf = bf(w_out), bf(w_ffn_gate), bf(w_ffn_up), bf(w_ffn_down)
    kt_mem, v_mem = _memkv(mem, ln_mem, bf(w_mem_kv))
    h = x.reshape(t, d)

    h = _mixer_a(h, kt_mem, v_mem, 0, row(ln_mix_pre[0]), bf(w_in_a[0]), w_spatial[0],
                 b_spatial[0].reshape(A_GROUPS, CHUNK, 1), row(ln_v_g[0]), row(ln_v_b[0]),
                 w_out_bf, row(ln_mix_post[0]), s, MIXER_A_TILE)
    h = _ffn(h, 0, row(ln_ffn_pre[0]), wg_bf, wu_bf, wd_bf, row(ln_ffn_post[0]), FFN_TILE)

    pad = LANES - FOX_HEADS
    w_f = jnp.pad(w_shared_kv[:, 2 * MAIN_WIDTH:], ((0, 0), (0, pad)))
    b_f = jnp.pad(b_forget, (0, pad)).reshape(1, LANES)
    kp, qt, vt, qm = _proj_b(
        h.reshape(b, s, d), row(ln_shared), row(ln_mix_pre[1]),
        bf(w_shared_kv[:, :MAIN_WIDTH]), bf(w_shared_kv[:, MAIN_WIDTH:2 * MAIN_WIDTH].T),
        bf(w_f), b_f, bf(w_in_b[0][:, :MAIN_WIDTH].T), bf(w_in_b[0][:, MAIN_WIDTH:]), PROJ_TILE)
    attn = _fox_attention(kp, qt, vt)
    h = _mixer_b(h, attn.reshape(t, MAIN_WIDTH), qm.reshape(t, MEM_WIDTH), kt_mem, v_mem, 1,
                 w_out_bf, row(ln_mix_post[1]), s, TOKEN_TILE)
    h = _ffn(h, 1, row(ln_ffn_pre[1]), wg_bf, wu_bf, wd_bf, row(ln_ffn_post[1]), FFN_TILE)
    return h.reshape(b, s, d)
```

```python
import functools
import math

import numpy as np
import jax
import jax.numpy as jnp
from jax import lax
from jax.experimental import pallas as pl
from jax.experimental.pallas import tpu as pltpu

F32 = jnp.float32
BF16 = jnp.bfloat16

D_MODEL = 1024
HEAD_DIM = 64
MEM_HEADS = 4
MEM_WIDTH = MEM_HEADS * HEAD_DIM
MAIN_WIDTH = D_MODEL - MEM_WIDTH
CHUNK = 128
A_GROUPS = 6
FOX_HEADS = MAIN_WIDTH // HEAD_DIM
RMS_EPS = 1e-6
LN_EPS = 1e-5
SCALE = HEAD_DIM ** -0.5
LOG2E = math.log2(math.e)
LANES = 128
VMEM_LIMIT = 56 * 1024 * 1024
TOKEN_TILE = 512
MIXER_A_TILE = 1024
MIXER_SUB_TILES = 4
PROJ_TILE = 1024
PROJ_SUB_TILES = 4
FFN_TILE = 1024
FFN_SUB_ROWS = 256

N_PIECES = 3
PIECE_STRIDE = 16
CT_ROW = 36
V_ROWS = 80
FOX_TK = 256
FOX_TQ = 512
STEPS_PER_ITER = 4

NT_DIMS = (((1,), (1,)), ((), ()))


def _dot(a, b):
    return jnp.dot(a, b, preferred_element_type=F32)


def _dot_nt(a, b):
    return lax.dot_general(a, b, NT_DIMS, preferred_element_type=F32)


def _rms(x, g):
    return (x * lax.rsqrt(jnp.mean(x * x, axis=-1, keepdims=True) + RMS_EPS)) * g


def _split_bf16(x):
    p1 = x.astype(BF16)
    r1 = x - p1.astype(F32)
    p2 = r1.astype(BF16)
    p3 = (r1 - p2.astype(F32)).astype(BF16)
    return p1, p2, p3


def _params(*sem):
    return pltpu.CompilerParams(dimension_semantics=sem, vmem_limit_bytes=VMEM_LIMIT)


def _full(shape):
    n = len(shape)
    return pl.BlockSpec(shape, lambda *_: (0,) * n)


def _cast_specs(params, layer, n_steps, flat_step):
    in_specs, out_specs, out_shapes = [], [], []
    for w in params:
        _, r, c = w.shape
        rows = r // n_steps
        assert rows * n_steps == r and rows % 16 == 0
        in_specs.append(pl.BlockSpec((1, rows, c), lambda *g: (layer, flat_step(*g), 0)))
        out_specs.append(pl.BlockSpec((rows, c), lambda *g: (flat_step(*g), 0)))
        out_shapes.append(jax.ShapeDtypeStruct((r, c), BF16))
    return in_specs, out_specs, out_shapes


def _cast_slabs(src_refs, dst_refs):
    for src, dst in zip(src_refs, dst_refs):
        dst[...] = src[0].astype(BF16)


def _memkv_kernel(mem_ref, g_ref, w_ref, kt_ref, v_ref):
    m = mem_ref.shape[1]
    n = _rms(mem_ref[0], g_ref[0])
    kv = _dot(n.astype(BF16), w_ref[0])
    k_t = kv[:, :MEM_WIDTH].T
    v = kv[:, MEM_WIDTH:]
    head_of_row = lax.broadcasted_iota(jnp.int32, (MEM_WIDTH, 1), 0) // HEAD_DIM
    head_of_lane = lax.broadcasted_iota(jnp.int32, (1, MEM_WIDTH), 1) // HEAD_DIM
    for hh in range(MEM_HEADS):
        kt_ref[0, 0, :, hh * m:(hh + 1) * m] = jnp.where(head_of_row == hh, k_t, 0.0).astype(BF16)
        v_ref[0, 0, hh * m:(hh + 1) * m, :] = jnp.where(head_of_lane == hh, v, 0.0).astype(BF16)


def _memkv(mem, ln_mem, w_mem_kv):
    depth = ln_mem.shape[0]
    b, m, d = mem.shape
    n_out = w_mem_kv.shape[-1]
    return pl.pallas_call(
        _memkv_kernel,
        grid=(depth, b),
        in_specs=[
            pl.BlockSpec((1, m, d), lambda l, i: (i, 0, 0)),
            pl.BlockSpec((1, 1, d), lambda l, i: (l, 0, 0)),
            pl.BlockSpec((1, d, n_out), lambda l, i: (l, 0, 0)),
        ],
        out_specs=[
            pl.BlockSpec((1, 1, MEM_WIDTH, MEM_HEADS * m), lambda l, i: (l, i, 0, 0)),
            pl.BlockSpec((1, 1, MEM_HEADS * m, MEM_WIDTH), lambda l, i: (l, i, 0, 0)),
        ],
        out_shape=[
            jax.ShapeDtypeStruct((depth, b, MEM_WIDTH, MEM_HEADS * m), BF16),
            jax.ShapeDtypeStruct((depth, b, MEM_HEADS * m, MEM_WIDTH), BF16),
        ],
        compiler_params=_params("arbitrary", "arbitrary"),
        name="memkv",
    )(mem, ln_mem.reshape(depth, 1, d), w_mem_kv)


def _memory_scores(qm, kt):
    return _dot(qm, kt)


def _memory_readout(s_all, v):
    m = s_all.shape[1] // MEM_HEADS
    probs = []
    for hh in range(MEM_HEADS):
        s = s_all[:, hh * m:(hh + 1) * m]
        p = jnp.exp(s - jnp.max(s, axis=-1, keepdims=True))
        inv = 1.0 / jnp.sum(p, axis=-1, keepdims=True)
        probs.append((p * inv).astype(BF16))
    return _dot(jnp.concatenate(probs, axis=1), v)


def _mixer_a_kernel(n_cast, h_ref, kt_ref, vm_ref, g_pre_ref, w_in_ref, wsp_ref, bsp_ref, lng_ref,
                    lnb_ref, w_out_ref, g_post_ref, *rest):
    cast_src, (o_ref, *cast_dst), mixed_ref = rest[:n_cast], rest[n_cast:-1], rest[-1]
    _cast_slabs(cast_src, cast_dst)
    sub = h_ref.shape[0] // MIXER_SUB_TILES
    n_chunks = sub // CHUNK
    row = lax.broadcasted_iota(jnp.int32, (CHUNK, CHUNK), 0)
    col = lax.broadcasted_iota(jnp.int32, (CHUNK, CHUNK), 1)
    causal = row >= col
    w_sp = [jnp.where(causal, wsp_ref[g], 0.0).astype(BF16) for g in range(A_GROUPS)]

    def in_proj(r):
        h = h_ref[r * sub:(r + 1) * sub, :]
        a = _rms(h, g_pre_ref[...]).astype(BF16)
        return h, _dot(a, w_in_ref[...])

    def mix(r, h, proj):
        base = r * sub
        s_mem = _memory_scores((proj[:, 2 * MAIN_WIDTH:] * SCALE).astype(BF16), kt_ref[0, 0])
        u = jax.nn.gelu(proj[:, :MAIN_WIDTH])
        v = jax.nn.gelu(proj[:, MAIN_WIDTH:2 * MAIN_WIDTH])
        mu = jnp.mean(v, axis=-1, keepdims=True)
        vc = v - mu
        vn = vc * lax.rsqrt(jnp.mean(vc * vc, axis=-1, keepdims=True) + LN_EPS)
        vn = (vn * lng_ref[...] + lnb_ref[...]).astype(BF16)
        for g in range(A_GROUPS):
            bias = bsp_ref[g]
            cols = slice(g * CHUNK, (g + 1) * CHUNK)
            v_g = jnp.concatenate([vn[c * CHUNK:(c + 1) * CHUNK, cols] for c in range(n_chunks)],
                                  axis=1)
            s_g = _dot(w_sp[g], v_g)
            for c in range(n_chunks):
                rows = slice(c * CHUNK, (c + 1) * CHUNK)
                s = s_g[:, c * CHUNK:(c + 1) * CHUNK] + bias
                mixed_ref[base + c * CHUNK:base + (c + 1) * CHUNK, cols] = (
                    u[rows, cols] * s).astype(BF16)
        mixed_ref[base:base + sub, MAIN_WIDTH:] = _memory_readout(s_mem, vm_ref[0, 0]).astype(BF16)
        y = _dot(mixed_ref[base:base + sub, :], w_out_ref[...])
        o_ref[base:base + sub, :] = h + _rms(y, g_post_ref[...])

    nxt = in_proj(0)
    for r in range(MIXER_SUB_TILES):
        cur = nxt
        if r + 1 < MIXER_SUB_TILES:
            nxt = in_proj(r + 1)
        mix(r, *cur)


def _mixer_a(h, kt, vm, layer, g_pre, w_in, wsp, bsp, lng, lnb, w_out, g_post, seq, tm,
             cast_params, cast_layer):
    t, d = h.shape
    per_batch = seq // tm
    c_in, c_out, c_shape = _cast_specs(cast_params, cast_layer, t // tm, lambda i: i)
    return pl.pallas_call(
        functools.partial(_mixer_a_kernel, len(cast_params)),
        grid=(t // tm,),
        in_specs=[
            pl.BlockSpec((tm, d), lambda i: (i, 0)),
            pl.BlockSpec((1, 1) + kt.shape[2:], lambda i: (layer, i // per_batch, 0, 0)),
            pl.BlockSpec((1, 1) + vm.shape[2:], lambda i: (layer, i // per_batch, 0, 0)),
            _full(g_pre.shape), _full(w_in.shape), _full(wsp.shape), _full(bsp.shape),
            _full(lng.shape), _full(lnb.shape), _full(w_out.shape), _full(g_post.shape),
        ] + c_in,
        out_specs=[pl.BlockSpec((tm, d), lambda i: (i, 0))] + c_out,
        out_shape=[jax.ShapeDtypeStruct((t, d), F32)] + c_shape,
        scratch_shapes=[pltpu.VMEM((tm, d), BF16)],
        compiler_params=_params("arbitrary"),
        name="mixer_a",
    )(h, kt, vm, g_pre, w_in, wsp, bsp, lng, lnb, w_out, g_post, *cast_params)


def _ffn_kernel(h_ref, g_pre_ref, wg_ref, wu_ref, wd_ref, g_post_ref, o_ref):
    sub = FFN_SUB_ROWS
    for r in range(h_ref.shape[0] // sub):
        rows = slice(r * sub, (r + 1) * sub)
        h = h_ref[rows, :]
        f = _rms(h, g_pre_ref[...]).astype(BF16)
        gate = _dot(f, wg_ref[...])
        up = _dot(f, wu_ref[...])
        act = (jax.nn.silu(gate) * up).astype(BF16)
        y = _dot(act, wd_ref[...])
        o_ref[rows, :] = h + _rms(y, g_post_ref[...])


def _ffn(h, g_pre, wg, wu, wd, g_post, tm):
    t, d = h.shape
    return pl.pallas_call(
        _ffn_kernel,
        grid=(t // tm,),
        in_specs=[
            pl.BlockSpec((tm, d), lambda i: (i, 0)),
            _full(g_pre.shape), _full(wg.shape), _full(wu.shape), _full(wd.shape),
            _full(g_post.shape),
        ],
        out_specs=pl.BlockSpec((tm, d), lambda i: (i, 0)),
        out_shape=jax.ShapeDtypeStruct((t, d), F32),
        compiler_params=_params("arbitrary"),
        name="ffn",
    )(h, g_pre, wg, wu, wd, g_post)


def _gate_selector():
    g = np.zeros((LANES, LANES), np.float32)
    for i in range(N_PIECES):
        for hd in range(FOX_HEADS):
            g[PIECE_STRIDE * i + hd, HEAD_DIM + N_PIECES * hd + i] = -1.0
    return jnp.asarray(g, dtype=BF16)


def _pack_pieces(pieces):
    packed = pieces[0].astype(F32)
    for i in range(1, N_PIECES):
        packed = packed + pltpu.roll(pieces[i].astype(F32), PIECE_STRIDE * i, axis=1)
    return packed.astype(BF16)


def _proj_b_kernel(n_cast, h_ref, g_sh_ref, g_pre_ref, w_k_ref, w_vt_ref, w_f_ref, b_f_ref,
                   w_qt_ref, w_qm_ref, gsel_ref, *rest):
    cast_src, carry_ref = rest[:n_cast], rest[-1]
    kp_ref, qt_ref, vt_ref, qm_ref, *cast_dst = rest[n_cast:-1]
    _cast_slabs(cast_src, cast_dst)
    sub = h_ref.shape[1] // PROJ_SUB_TILES

    @pl.when(pl.program_id(1) == 0)
    def _():
        carry_ref[...] = jnp.zeros_like(carry_ref)

    lane = lax.broadcasted_iota(jnp.int32, (1, LANES), 1)
    row = lax.broadcasted_iota(jnp.int32, (sub, sub), 0)
    col = lax.broadcasted_iota(jnp.int32, (sub, sub), 1)
    tri = (row >= col).astype(BF16)
    low = lane < HEAD_DIM
    ones_lanes = jnp.where((lane >= HEAD_DIM + CT_ROW) & (lane < HEAD_DIM + CT_ROW + N_PIECES),
                           1.0, 0.0)
    rid = lax.broadcasted_iota(jnp.int32, (HEAD_DIM, 1), 0)
    rid_v = lax.broadcasted_iota(jnp.int32, (V_ROWS - HEAD_DIM, sub), 0)
    ones_row = jnp.where(rid_v == 0, 1.0, 0.0).astype(BF16)

    def project(r):
        rows = slice(r * sub, (r + 1) * sub)
        h = h_ref[0, rows, :]
        hn = h * lax.rsqrt(jnp.mean(h * h, axis=-1, keepdims=True) + RMS_EPS)
        s_in = (hn * g_sh_ref[...]).astype(BF16)
        a = (hn * g_pre_ref[...]).astype(BF16)
        log_f = jax.nn.log_sigmoid(_dot(s_in, w_f_ref[...]) + b_f_ref[...])
        log_f = jnp.where(lane < FOX_HEADS, log_f, 0.0)
        k = _dot(s_in, w_k_ref[...])
        qt = _dot_nt(w_qt_ref[...], a) * (SCALE * LOG2E)
        vt = _dot_nt(w_vt_ref[...], s_in)
        qm_ref[0, rows, :] = (_dot(a, w_qm_ref[...]) * SCALE).astype(BF16)
        return log_f, k, qt, vt

    def assemble(r, log_f, k, qt, vt):
        rows = slice(r * sub, (r + 1) * sub)
        sums = _dot(tri, _pack_pieces(_split_bf16(log_f)))
        c = sums + carry_ref[...]
        for i in range(1, N_PIECES):
            c = c + pltpu.roll(sums, LANES - PIECE_STRIDE * i, axis=1)
        c = jnp.where(lane < FOX_HEADS, c, 0.0)
        carry_ref[...] = c[sub - 1:sub, :]
        c2 = c * LOG2E

        aug = _dot(_pack_pieces(_split_bf16(c2)), gsel_ref[...]) + ones_lanes
        for p in range(FOX_HEADS // 2):
            blk = k[:, p * LANES:(p + 1) * LANES]
            kp_ref[0, 2 * p, rows, :] = jnp.where(low, blk, aug).astype(BF16)
            kp_ref[0, 2 * p + 1, rows, :] = jnp.where(
                low, pltpu.roll(blk, HEAD_DIM, axis=1), aug).astype(BF16)

        t1, t2, t3 = _split_bf16(c2.T)
        for hd in range(FOX_HEADS):
            sel = jnp.where((rid >= N_PIECES * hd) & (rid < N_PIECES * (hd + 1)), 1.0, 0.0)
            aug_q = jnp.where(rid == CT_ROW, t1[hd:hd + 1, :].astype(F32),
                              jnp.where(rid == CT_ROW + 1, t2[hd:hd + 1, :].astype(F32),
                                        jnp.where(rid == CT_ROW + 2, t3[hd:hd + 1, :].astype(F32),
                                                  sel)))
            qt_ref[0, hd * LANES:hd * LANES + HEAD_DIM, rows] = (
                qt[hd * HEAD_DIM:(hd + 1) * HEAD_DIM, :].astype(BF16))
            qt_ref[0, hd * LANES + HEAD_DIM:(hd + 1) * LANES, rows] = aug_q.astype(BF16)
            vt_ref[0, hd, :HEAD_DIM, rows] = vt[hd * HEAD_DIM:(hd + 1) * HEAD_DIM, :].astype(BF16)
            vt_ref[0, hd, HEAD_DIM:, rows] = ones_row

    nxt = project(0)
    for r in range(PROJ_SUB_TILES):
        cur = nxt
        if r + 1 < PROJ_SUB_TILES:
            nxt = project(r + 1)
        assemble(r, *cur)


def _proj_b(h3, g_sh, g_pre, w_k, w_vt, w_f, b_f, w_qt, w_qm, tm, cast_params, cast_layer):
    b, s, d = h3.shape
    gsel = _gate_selector()
    per_batch = s // tm
    c_in, c_out, c_shape = _cast_specs(cast_params, cast_layer, b * per_batch,
                                       lambda i, j: i * per_batch + j)
    return pl.pallas_call(
        functools.partial(_proj_b_kernel, len(cast_params)),
        grid=(b, per_batch),
        in_specs=[pl.BlockSpec((1, tm, d), lambda i, j: (i, j, 0)),
                  _full(g_sh.shape), _full(g_pre.shape), _full(w_k.shape), _full(w_vt.shape),
                  _full(w_f.shape), _full(b_f.shape), _full(w_qt.shape), _full(w_qm.shape),
                  _full(gsel.shape)] + c_in,
        out_specs=[
            pl.BlockSpec((1, FOX_HEADS, tm, LANES), lambda i, j: (i, 0, j, 0)),
            pl.BlockSpec((1, FOX_HEADS * LANES, tm), lambda i, j: (i, 0, j)),
            pl.BlockSpec((1, FOX_HEADS, V_ROWS, tm), lambda i, j: (i, 0, 0, j)),
            pl.BlockSpec((1, tm, MEM_WIDTH), lambda i, j: (i, j, 0)),
        ] + c_out,
        out_shape=[
            jax.ShapeDtypeStruct((b, FOX_HEADS, s, LANES), BF16),
            jax.ShapeDtypeStruct((b, FOX_HEADS * LANES, s), BF16),
            jax.ShapeDtypeStruct((b, FOX_HEADS, V_ROWS, s), BF16),
            jax.ShapeDtypeStruct((b, s, MEM_WIDTH), BF16),
        ] + c_shape,
        scratch_shapes=[pltpu.VMEM((1, LANES), F32)],
        compiler_params=_params("arbitrary", "arbitrary"),
        name="proj_b",
    )(h3, g_sh, g_pre, w_k, w_vt, w_f, b_f, w_qt, w_qm, gsel, *cast_params)


def _fox_kernel(k_ref, qt_ref, vt_ref, o_ref, s0_ref, s1_ref, bm0_ref, bm1_ref,
                p0_ref, p1_ref, a0_ref, a1_ref, bias_ref, m_ref, acc_ref):
    tk = FOX_TK
    tq = FOX_TQ
    n_q = k_ref.shape[2] // tq
    assert tq == 2 * tk and n_q >= 2 and n_q % 2 == 0
    sub = 2
    s_refs = (s0_ref, s1_ref)
    bm_refs = (bm0_ref, bm1_ref)
    p_refs = (p0_ref, p1_ref)
    a_refs = (a0_ref, a1_ref)

    def issue_scores(par, qi, kb, with_max):
        k0 = pl.multiple_of(kb * tk, tk)
        q0 = pl.multiple_of(qi * tq, tq)
        for hh in range(2):
            s = _dot(k_ref[0, hh, pl.ds(k0, tk), :],
                     qt_ref[0, hh * LANES:(hh + 1) * LANES, pl.ds(q0, tq)])
            s_refs[par][hh] = s
            if with_max:
                bm_refs[par][hh] = jnp.max(s, axis=0, keepdims=True)

    def accumulate(par, qi, kb):
        k0 = pl.multiple_of(kb * tk, tk)
        q0 = pl.multiple_of(qi * tq, tq)
        for hh in range(2):
            pv = _dot(vt_ref[0, hh, :, pl.ds(k0, tk)], p_refs[par][hh])
            acc_ref[hh, :, pl.ds(q0, tq)] = a_refs[par][hh] * acc_ref[hh, :, pl.ds(q0, tq)] + pv

    def softmax(idx, qi, mask_id):
        q0 = pl.multiple_of(qi * tq, tq)
        par = idx % 2
        for hh in range(2):
            s = s_refs[par][hh]
            if mask_id is None:
                m_blk = bm_refs[par][hh]
            else:
                s = s + bias_ref[mask_id]
                m_blk = jnp.max(s, axis=0, keepdims=True)
            m_old = m_ref[hh, :, pl.ds(q0, tq)]
            m_new = jnp.maximum(m_old, m_blk)
            m_ref[hh, :, pl.ds(q0, tq)] = m_new
            a_refs[par][hh] = jnp.exp2(m_old - m_new)
            p_refs[par][hh] = jnp.exp2(s - m_new).astype(BF16)

    def finalize(qi):
        q0 = pl.multiple_of(qi * tq, tq)
        halves = [acc_ref[hh, :HEAD_DIM, pl.ds(q0, tq)]
                  / acc_ref[hh, HEAD_DIM:HEAD_DIM + 1, pl.ds(q0, tq)] for hh in range(2)]
        o_ref[0, pl.ds(q0, tq), :] = jnp.concatenate(halves, axis=0).T.astype(BF16)

    def step(idx, prev, cur, nxt, mask_id, after_accumulate=None):
        accumulate((idx + 1) % 2, *prev)
        if after_accumulate is not None:
            after_accumulate()
        issue_scores((idx + 1) % 2, *nxt, with_max=mask_id is None)
        softmax(idx, cur[0], mask_id)

    key_pos = lax.broadcasted_iota(jnp.int32, (tk, tq), 0)
    qry_pos = lax.broadcasted_iota(jnp.int32, (tk, tq), 1)
    for d in range(sub):
        bias_ref[d] = jnp.where(key_pos + d * tk <= qry_pos, 0.0, -jnp.inf)
    m_ref[...] = jnp.full(m_ref.shape, -jnp.inf, F32)
    acc_ref[...] = jnp.zeros(acc_ref.shape, F32)
    p_refs[1][...] = jnp.zeros(p_refs[1].shape, BF16)
    a_refs[1][...] = jnp.zeros(a_refs[1].shape, F32)

    def below_next(qi, kb):
        more = kb + 1 < qi * sub
        wrap = qi + 1 < n_q
        return (jnp.where(more, qi, jnp.where(wrap, qi + 1, 0)),
                jnp.where(more, kb + 1, 0))

    issue_scores(0, 1, 0, with_max=True)

    def below_body(_, carry):
        prev, cur = carry[:2], carry[2:]
        for idx in range(STEPS_PER_ITER):
            nxt = below_next(*cur)
            step(idx, prev, cur, nxt, None)
            prev, cur = cur, nxt
        return prev + cur

    n_below = sub * n_q * (n_q - 1) // 2
    assert STEPS_PER_ITER == 2 * sub and n_below % STEPS_PER_ITER == 0
    zero = jnp.int32(0)
    carry = lax.fori_loop(0, n_below // STEPS_PER_ITER, below_body,
                          (zero, zero, jnp.int32(1), zero))

    def diag_body(j, prev):
        qa = 2 * j
        qb = qa + 1
        ka = qa * sub
        kb = qb * sub
        step(0, prev, (qa, ka), (qa, ka + 1), 0)
        step(1, (qa, ka), (qa, ka + 1), (qb, kb), 1)
        step(2, (qa, ka + 1), (qb, kb), (qb, kb + 1), 0, after_accumulate=lambda: finalize(qa))
        qn = jnp.minimum(qb + 1, n_q - 1)
        step(3, (qb, kb), (qb, kb + 1), (qn, qn * sub), 1)

        @pl.when(j >= 1)
        def _():
            finalize(qa - 1)

        return (qb, kb + 1)

    prev = lax.fori_loop(0, n_q // 2, diag_body, carry[:2])
    accumulate(1, *prev)
    finalize(n_q - 1)


def _fox_attention(kp, qt, vt):
    b, _, s, _ = kp.shape
    pairs = FOX_HEADS // 2
    return pl.pallas_call(
        _fox_kernel,
        grid=(b, pairs),
        in_specs=[
            pl.BlockSpec((1, 2, s, LANES), lambda i, p: (i, p, 0, 0)),
            pl.BlockSpec((1, 2 * LANES, s), lambda i, p: (i, p, 0)),
            pl.BlockSpec((1, 2, V_ROWS, s), lambda i, p: (i, p, 0, 0)),
        ],
        out_specs=pl.BlockSpec((1, s, LANES), lambda i, p: (i, 0, p)),
        out_shape=jax.ShapeDtypeStruct((b, s, MAIN_WIDTH), BF16),
        scratch_shapes=[
            pltpu.VMEM((2, FOX_TK, FOX_TQ), F32),
            pltpu.VMEM((2, FOX_TK, FOX_TQ), F32),
            pltpu.VMEM((2, 1, FOX_TQ), F32),
            pltpu.VMEM((2, 1, FOX_TQ), F32),
            pltpu.VMEM((2, FOX_TK, FOX_TQ), BF16),
            pltpu.VMEM((2, FOX_TK, FOX_TQ), BF16),
            pltpu.VMEM((2, 1, FOX_TQ), F32),
            pltpu.VMEM((2, 1, FOX_TQ), F32),
            pltpu.VMEM((2, FOX_TK, FOX_TQ), F32),
            pltpu.VMEM((2, 1, s), F32),
            pltpu.VMEM((2, V_ROWS, s), F32),
        ],
        compiler_params=_params("arbitrary", "arbitrary"),
        name="fox_attention",
    )(kp, qt, vt)


def _mixer_b_kernel(h_ref, attn_ref, qm_ref, kt_ref, vm_ref, w_out_ref, g_post_ref, o_ref):
    h = h_ref[...]
    s_mem = _memory_scores(qm_ref[...], kt_ref[0, 0])
    y_main = _dot(attn_ref[...], w_out_ref[:MAIN_WIDTH, :])
    mem_o = _memory_readout(s_mem, vm_ref[0, 0]).astype(BF16)
    y = y_main + _dot(mem_o, w_out_ref[MAIN_WIDTH:, :])
    o_ref[...] = h + _rms(y, g_post_ref[...])


def _mixer_b(h, attn, qm, kt, vm, layer, w_out, g_post, seq, tm):
    t, d = h.shape
    per_batch = seq // tm
    tile = lambda w: pl.BlockSpec((tm, w), lambda i: (i, 0))
    return pl.pallas_call(
        _mixer_b_kernel,
        grid=(t // tm,),
        in_specs=[tile(d), tile(MAIN_WIDTH), tile(MEM_WIDTH),
                  pl.BlockSpec((1, 1) + kt.shape[2:], lambda i: (layer, i // per_batch, 0, 0)),
                  pl.BlockSpec((1, 1) + vm.shape[2:], lambda i: (layer, i // per_batch, 0, 0)),
                  _full(w_out.shape), _full(g_post.shape)],
        out_specs=tile(d),
        out_shape=jax.ShapeDtypeStruct((t, d), F32),
        compiler_params=_params("arbitrary"),
        name="mixer_b",
    )(h, attn, qm, kt, vm, w_out, g_post)


def kernel(x, mem, ln_mix_pre, ln_mix_post, ln_ffn_pre, ln_ffn_post, ln_mem, w_mem_kv, w_out,
           w_ffn_gate, w_ffn_up, w_ffn_down, w_in_a, w_spatial, b_spatial, ln_v_g, ln_v_b,
           ln_shared, w_shared_kv, b_forget, w_in_b):
    b, s, d = x.shape
    t = b * s
    row = lambda g: g.reshape(1, -1)
    bf = lambda w: w.astype(BF16)

    kt_mem, v_mem = _memkv(mem, ln_mem, bf(w_mem_kv))
    h = x.reshape(t, d)

    ffn_weights = (w_ffn_gate, w_ffn_up, w_ffn_down)
    h, wg, wu, wd = _mixer_a(
        h, kt_mem, v_mem, 0, row(ln_mix_pre[0]), bf(w_in_a[0]), w_spatial[0],
        b_spatial[0].reshape(A_GROUPS, CHUNK, 1), row(ln_v_g[0]), row(ln_v_b[0]),
        bf(w_out[0]), row(ln_mix_post[0]), s, MIXER_A_TILE, ffn_weights, 0)
    h = _ffn(h, row(ln_ffn_pre[0]), wg, wu, wd, row(ln_ffn_post[0]), FFN_TILE)

    pad = LANES - FOX_HEADS
    w_f = jnp.pad(w_shared_kv[:, 2 * MAIN_WIDTH:], ((0, 0), (0, pad)))
    b_f = jnp.pad(b_forget, (0, pad)).reshape(1, LANES)
    kp, qt, vt, qm, wg, wu, wd, w_out1 = _proj_b(
        h.reshape(b, s, d), row(ln_shared), row(ln_mix_pre[1]),
        bf(w_shared_kv[:, :MAIN_WIDTH]), bf(w_shared_kv[:, MAIN_WIDTH:2 * MAIN_WIDTH].T),
        bf(w_f), b_f, bf(w_in_b[0][:, :MAIN_WIDTH].T), bf(w_in_b[0][:, MAIN_WIDTH:]), PROJ_TILE,
        ffn_weights + (w_out,), 1)
    attn = _fox_attention(kp, qt, vt)
    h = _mixer_b(h, attn.reshape(t, MAIN_WIDTH), qm.reshape(t, MEM_WIDTH), kt_mem, v_mem, 1,
                 w_out1, row(ln_mix_post[1]), s, TOKEN_TILE)
    h = _ffn(h, row(ln_ffn_pre[1]), wg, wu, wd, row(ln_ffn_post[1]), FFN_TILE)
    return h.reshape(b, s, d)
```

```python
import functools
import math

import numpy as np
import jax
import jax.numpy as jnp
from jax import lax
from jax.experimental import pallas as pl
from jax.experimental.pallas import tpu as pltpu

F32 = jnp.float32
BF16 = jnp.bfloat16

D_MODEL = 1024
HEAD_DIM = 64
MEM_HEADS = 4
MEM_WIDTH = MEM_HEADS * HEAD_DIM
MAIN_WIDTH = D_MODEL - MEM_WIDTH
CHUNK = 128
A_GROUPS = 6
FOX_HEADS = MAIN_WIDTH // HEAD_DIM
RMS_EPS = 1e-6
LN_EPS = 1e-5
SCALE = HEAD_DIM ** -0.5
LOG2E = math.log2(math.e)
LANES = 128
VMEM_LIMIT = 56 * 1024 * 1024
TOKEN_TILE = 1024
MIXER_A_TILE = 1024
MIXER_SUB_TILES = 4
PROJ_TILE = 1024
PROJ_SUB_TILES = 4
FFN_TILE = 1024
FFN_SUB_ROWS = 256

N_PIECES = 3
PIECE_STRIDE = 16
CT_ROW = 36
V_ROWS = 80
FOX_TK = 256
FOX_TQ = 256
STEPS_PER_ITER = 8

NT_DIMS = (((1,), (1,)), ((), ()))


def _dot(a, b):
    return jnp.dot(a, b, preferred_element_type=F32)


def _dot_nt(a, b):
    return lax.dot_general(a, b, NT_DIMS, preferred_element_type=F32)


def _rms(x, g):
    return (x * lax.rsqrt(jnp.mean(x * x, axis=-1, keepdims=True) + RMS_EPS)) * g


def _split_bf16(x):
    p1 = x.astype(BF16)
    r1 = x - p1.astype(F32)
    p2 = r1.astype(BF16)
    p3 = (r1 - p2.astype(F32)).astype(BF16)
    return p1, p2, p3


def _params(*sem):
    return pltpu.CompilerParams(dimension_semantics=sem, vmem_limit_bytes=VMEM_LIMIT)


def _full(shape):
    n = len(shape)
    return pl.BlockSpec(shape, lambda *_: (0,) * n)


def _cast_specs(params, layer, n_steps, flat_step):
    in_specs, out_specs, out_shapes = [], [], []
    for w in params:
        _, r, c = w.shape
        rows = r // n_steps
        assert rows * n_steps == r and rows % 16 == 0
        in_specs.append(pl.BlockSpec((1, rows, c), lambda *g: (layer, flat_step(*g), 0)))
        out_specs.append(pl.BlockSpec((rows, c), lambda *g: (flat_step(*g), 0)))
        out_shapes.append(jax.ShapeDtypeStruct((r, c), BF16))
    return in_specs, out_specs, out_shapes


def _cast_slabs(src_refs, dst_refs):
    for src, dst in zip(src_refs, dst_refs):
        dst[...] = src[0].astype(BF16)


def _memkv_kernel(mem_ref, g_ref, w_ref, kt_ref, v_ref):
    m = mem_ref.shape[1]
    n = _rms(mem_ref[0], g_ref[0])
    kv = _dot(n.astype(BF16), w_ref[0].astype(BF16))
    k_t = kv[:, :MEM_WIDTH].T
    v = kv[:, MEM_WIDTH:]
    head_of_row = lax.broadcasted_iota(jnp.int32, (MEM_WIDTH, 1), 0) // HEAD_DIM
    head_of_lane = lax.broadcasted_iota(jnp.int32, (1, MEM_WIDTH), 1) // HEAD_DIM
    for hh in range(MEM_HEADS):
        kt_ref[0, 0, :, hh * m:(hh + 1) * m] = jnp.where(head_of_row == hh, k_t, 0.0).astype(BF16)
        v_ref[0, 0, hh * m:(hh + 1) * m, :] = jnp.where(head_of_lane == hh, v, 0.0).astype(BF16)


def _memkv(mem, ln_mem, w_mem_kv):
    depth = ln_mem.shape[0]
    b, m, d = mem.shape
    n_out = w_mem_kv.shape[-1]
    return pl.pallas_call(
        _memkv_kernel,
        grid=(depth, b),
        in_specs=[
            pl.BlockSpec((1, m, d), lambda l, i: (i, 0, 0)),
            pl.BlockSpec((1, 1, d), lambda l, i: (l, 0, 0)),
            pl.BlockSpec((1, d, n_out), lambda l, i: (l, 0, 0)),
        ],
        out_specs=[
            pl.BlockSpec((1, 1, MEM_WIDTH, MEM_HEADS * m), lambda l, i: (l, i, 0, 0)),
            pl.BlockSpec((1, 1, MEM_HEADS * m, MEM_WIDTH), lambda l, i: (l, i, 0, 0)),
        ],
        out_shape=[
            jax.ShapeDtypeStruct((depth, b, MEM_WIDTH, MEM_HEADS * m), BF16),
            jax.ShapeDtypeStruct((depth, b, MEM_HEADS * m, MEM_WIDTH), BF16),
        ],
        compiler_params=_params("arbitrary", "arbitrary"),
        name="memkv",
    )(mem, ln_mem.reshape(depth, 1, d), w_mem_kv)


def _memory_scores(qm, kt):
    return _dot(qm, kt)


def _memory_readout(s_all, v):
    m = s_all.shape[1] // MEM_HEADS
    probs = []
    for hh in range(MEM_HEADS):
        s = s_all[:, hh * m:(hh + 1) * m]
        p = jnp.exp(s - jnp.max(s, axis=-1, keepdims=True))
        inv = 1.0 / jnp.sum(p, axis=-1, keepdims=True)
        probs.append((p * inv).astype(BF16))
    return _dot(jnp.concatenate(probs, axis=1), v)


def _mixer_a_kernel(n_cast, h_ref, kt_ref, vm_ref, g_pre_ref, w_in_ref, wsp_ref, bsp_ref, lng_ref,
                    lnb_ref, w_out_ref, g_post_ref, *rest):
    cast_src, (o_ref, *cast_dst), mixed_ref = rest[:n_cast], rest[n_cast:-1], rest[-1]
    _cast_slabs(cast_src, cast_dst)
    sub = h_ref.shape[0] // MIXER_SUB_TILES
    n_chunks = sub // CHUNK
    row = lax.broadcasted_iota(jnp.int32, (CHUNK, CHUNK), 0)
    col = lax.broadcasted_iota(jnp.int32, (CHUNK, CHUNK), 1)
    causal = row >= col
    w_sp = [jnp.where(causal, wsp_ref[g], 0.0).astype(BF16) for g in range(A_GROUPS)]

    def in_proj(r):
        h = h_ref[r * sub:(r + 1) * sub, :]
        a = _rms(h, g_pre_ref[...]).astype(BF16)
        return h, _dot(a, w_in_ref[...])

    def mix(r, h, proj):
        base = r * sub
        s_mem = _memory_scores((proj[:, 2 * MAIN_WIDTH:] * SCALE).astype(BF16), kt_ref[0, 0])
        u = jax.nn.gelu(proj[:, :MAIN_WIDTH])
        v = jax.nn.gelu(proj[:, MAIN_WIDTH:2 * MAIN_WIDTH])
        mu = jnp.mean(v, axis=-1, keepdims=True)
        vc = v - mu
        vn = vc * lax.rsqrt(jnp.mean(vc * vc, axis=-1, keepdims=True) + LN_EPS)
        vn = (vn * lng_ref[...] + lnb_ref[...]).astype(BF16)
        for g in range(A_GROUPS):
            bias = bsp_ref[g]
            cols = slice(g * CHUNK, (g + 1) * CHUNK)
            v_g = jnp.concatenate([vn[c * CHUNK:(c + 1) * CHUNK, cols] for c in range(n_chunks)],
                                  axis=1)
            s_g = _dot(w_sp[g], v_g)
            for c in range(n_chunks):
                rows = slice(c * CHUNK, (c + 1) * CHUNK)
                s = s_g[:, c * CHUNK:(c + 1) * CHUNK] + bias
                mixed_ref[base + c * CHUNK:base + (c + 1) * CHUNK, cols] = (
                    u[rows, cols] * s).astype(BF16)
        mixed_ref[base:base + sub, MAIN_WIDTH:] = _memory_readout(s_mem, vm_ref[0, 0]).astype(BF16)
        y = _dot(mixed_ref[base:base + sub, :], w_out_ref[...])
        o_ref[base:base + sub, :] = h + _rms(y, g_post_ref[...])

    nxt = in_proj(0)
    for r in range(MIXER_SUB_TILES):
        cur = nxt
        if r + 1 < MIXER_SUB_TILES:
            nxt = in_proj(r + 1)
        mix(r, *cur)


def _mixer_a(h, kt, vm, layer, g_pre, w_in, wsp, bsp, lng, lnb, w_out, g_post, seq, tm,
             cast_params, cast_layer):
    t, d = h.shape
    per_batch = seq // tm
    c_in, c_out, c_shape = _cast_specs(cast_params, cast_layer, t // tm, lambda i: i)
    return pl.pallas_call(
        functools.partial(_mixer_a_kernel, len(cast_params)),
        grid=(t // tm,),
        in_specs=[
            pl.BlockSpec((tm, d), lambda i: (i, 0)),
            pl.BlockSpec((1, 1) + kt.shape[2:], lambda i: (layer, i // per_batch, 0, 0)),
            pl.BlockSpec((1, 1) + vm.shape[2:], lambda i: (layer, i // per_batch, 0, 0)),
            _full(g_pre.shape), _full(w_in.shape), _full(wsp.shape), _full(bsp.shape),
            _full(lng.shape), _full(lnb.shape), _full(w_out.shape), _full(g_post.shape),
        ] + c_in,
        out_specs=[pl.BlockSpec((tm, d), lambda i: (i, 0))] + c_out,
        out_shape=[jax.ShapeDtypeStruct((t, d), F32)] + c_shape,
        scratch_shapes=[pltpu.VMEM((tm, d), BF16)],
        compiler_params=_params("arbitrary"),
        name="mixer_a",
    )(h, kt, vm, g_pre, w_in, wsp, bsp, lng, lnb, w_out, g_post, *cast_params)


def _ffn_kernel(h_ref, g_pre_ref, wg_ref, wu_ref, wd_ref, g_post_ref, o_ref):
    sub = FFN_SUB_ROWS
    for r in range(h_ref.shape[0] // sub):
        rows = slice(r * sub, (r + 1) * sub)
        h = h_ref[rows, :]
        f = _rms(h, g_pre_ref[...]).astype(BF16)
        gate = _dot(f, wg_ref[...])
        up = _dot(f, wu_ref[...])
        act = (jax.nn.silu(gate) * up).astype(BF16)
        y = _dot(act, wd_ref[...])
        o_ref[rows, :] = h + _rms(y, g_post_ref[...])


def _ffn(h, g_pre, wg, wu, wd, g_post, tm):
    t, d = h.shape
    return pl.pallas_call(
        _ffn_kernel,
        grid=(t // tm,),
        in_specs=[
            pl.BlockSpec((tm, d), lambda i: (i, 0)),
            _full(g_pre.shape), _full(wg.shape), _full(wu.shape), _full(wd.shape),
            _full(g_post.shape),
        ],
        out_specs=pl.BlockSpec((tm, d), lambda i: (i, 0)),
        out_shape=jax.ShapeDtypeStruct((t, d), F32),
        compiler_params=_params("arbitrary"),
        name="ffn",
    )(h, g_pre, wg, wu, wd, g_post)


def _gate_selector():
    g = np.zeros((LANES, LANES), np.float32)
    for i in range(N_PIECES):
        for hd in range(FOX_HEADS):
            g[PIECE_STRIDE * i + hd, HEAD_DIM + N_PIECES * hd + i] = -1.0
    return jnp.asarray(g, dtype=BF16)


def _pack_pieces(pieces):
    packed = pieces[0].astype(F32)
    for i in range(1, N_PIECES):
        packed = packed + pltpu.roll(pieces[i].astype(F32), PIECE_STRIDE * i, axis=1)
    return packed.astype(BF16)


def _proj_b_kernel(n_cast, h_ref, g_sh_ref, g_pre_ref, w_k_ref, w_vt_ref, w_f_ref, b_f_ref,
                   w_qt_ref, w_qm_ref, gsel_ref, *rest):
    cast_src, carry_ref = rest[:n_cast], rest[-1]
    kp_ref, qt_ref, vt_ref, qm_ref, *cast_dst = rest[n_cast:-1]
    _cast_slabs(cast_src, cast_dst)
    sub = h_ref.shape[1] // PROJ_SUB_TILES

    @pl.when(pl.program_id(1) == 0)
    def _():
        carry_ref[...] = jnp.zeros_like(carry_ref)

    lane = lax.broadcasted_iota(jnp.int32, (1, LANES), 1)
    row = lax.broadcasted_iota(jnp.int32, (sub, sub), 0)
    col = lax.broadcasted_iota(jnp.int32, (sub, sub), 1)
    tri = (row >= col).astype(BF16)
    low = lane < HEAD_DIM
    ones_lanes = jnp.where((lane >= HEAD_DIM + CT_ROW) & (lane < HEAD_DIM + CT_ROW + N_PIECES),
                           1.0, 0.0)
    rid = lax.broadcasted_iota(jnp.int32, (HEAD_DIM, 1), 0)
    rid_v = lax.broadcasted_iota(jnp.int32, (V_ROWS - HEAD_DIM, sub), 0)
    ones_row = jnp.where(rid_v == 0, 1.0, 0.0).astype(BF16)

    def project(r):
        rows = slice(r * sub, (r + 1) * sub)
        h = h_ref[0, rows, :]
        hn = h * lax.rsqrt(jnp.mean(h * h, axis=-1, keepdims=True) + RMS_EPS)
        s_in = (hn * g_sh_ref[...]).astype(BF16)
        a = (hn * g_pre_ref[...]).astype(BF16)
        log_f = jax.nn.log_sigmoid(_dot(s_in, w_f_ref[...]) + b_f_ref[...])
        log_f = jnp.where(lane < FOX_HEADS, log_f, 0.0)
        k = _dot(s_in, w_k_ref[...])
        qt = _dot_nt(w_qt_ref[...], a) * (SCALE * LOG2E)
        vt = _dot_nt(w_vt_ref[...], s_in)
        qm_ref[0, rows, :] = (_dot(a, w_qm_ref[...]) * SCALE).astype(BF16)
        return log_f, k, qt, vt

    def assemble(r, log_f, k, qt, vt):
        rows = slice(r * sub, (r + 1) * sub)
        sums = _dot(tri, _pack_pieces(_split_bf16(log_f)))
        c = sums + carry_ref[...]
        for i in range(1, N_PIECES):
            c = c + pltpu.roll(sums, LANES - PIECE_STRIDE * i, axis=1)
        c = jnp.where(lane < FOX_HEADS, c, 0.0)
        carry_ref[...] = c[sub - 1:sub, :]
        c2 = c * LOG2E

        aug = _dot(_pack_pieces(_split_bf16(c2)), gsel_ref[...]) + ones_lanes
        for p in range(FOX_HEADS // 2):
            blk = k[:, p * LANES:(p + 1) * LANES]
            kp_ref[0, 2 * p, rows, :] = jnp.where(low, blk, aug).astype(BF16)
            kp_ref[0, 2 * p + 1, rows, :] = jnp.where(
                low, pltpu.roll(blk, HEAD_DIM, axis=1), aug).astype(BF16)

        t1, t2, t3 = _split_bf16(c2.T)
        for hd in range(FOX_HEADS):
            sel = jnp.where((rid >= N_PIECES * hd) & (rid < N_PIECES * (hd + 1)), 1.0, 0.0)
            aug_q = jnp.where(rid == CT_ROW, t1[hd:hd + 1, :].astype(F32),
                              jnp.where(rid == CT_ROW + 1, t2[hd:hd + 1, :].astype(F32),
                                        jnp.where(rid == CT_ROW + 2, t3[hd:hd + 1, :].astype(F32),
                                                  sel)))
            qt_ref[0, hd * LANES:hd * LANES + HEAD_DIM, rows] = (
                qt[hd * HEAD_DIM:(hd + 1) * HEAD_DIM, :].astype(BF16))
            qt_ref[0, hd * LANES + HEAD_DIM:(hd + 1) * LANES, rows] = aug_q.astype(BF16)
            vt_ref[0, hd, :HEAD_DIM, rows] = vt[hd * HEAD_DIM:(hd + 1) * HEAD_DIM, :].astype(BF16)
            vt_ref[0, hd, HEAD_DIM:, rows] = ones_row

    nxt = project(0)
    for r in range(PROJ_SUB_TILES):
        cur = nxt
        if r + 1 < PROJ_SUB_TILES:
            nxt = project(r + 1)
        assemble(r, *cur)


def _proj_b(h3, g_sh, g_pre, w_k, w_vt, w_f, b_f, w_qt, w_qm, tm, cast_params, cast_layer):
    b, s, d = h3.shape
    gsel = _gate_selector()
    per_batch = s // tm
    c_in, c_out, c_shape = _cast_specs(cast_params, cast_layer, b * per_batch,
                                       lambda i, j: i * per_batch + j)
    return pl.pallas_call(
        functools.partial(_proj_b_kernel, len(cast_params)),
        grid=(b, per_batch),
        in_specs=[pl.BlockSpec((1, tm, d), lambda i, j: (i, j, 0)),
                  _full(g_sh.shape), _full(g_pre.shape), _full(w_k.shape), _full(w_vt.shape),
                  _full(w_f.shape), _full(b_f.shape), _full(w_qt.shape), _full(w_qm.shape),
                  _full(gsel.shape)] + c_in,
        out_specs=[
            pl.BlockSpec((1, FOX_HEADS, tm, LANES), lambda i, j: (i, 0, j, 0)),
            pl.BlockSpec((1, FOX_HEADS * LANES, tm), lambda i, j: (i, 0, j)),
            pl.BlockSpec((1, FOX_HEADS, V_ROWS, tm), lambda i, j: (i, 0, 0, j)),
            pl.BlockSpec((1, tm, MEM_WIDTH), lambda i, j: (i, j, 0)),
        ] + c_out,
        out_shape=[
            jax.ShapeDtypeStruct((b, FOX_HEADS, s, LANES), BF16),
            jax.ShapeDtypeStruct((b, FOX_HEADS * LANES, s), BF16),
            jax.ShapeDtypeStruct((b, FOX_HEADS, V_ROWS, s), BF16),
            jax.ShapeDtypeStruct((b, s, MEM_WIDTH), BF16),
        ] + c_shape,
        scratch_shapes=[pltpu.VMEM((1, LANES), F32)],
        compiler_params=_params("arbitrary", "arbitrary"),
        name="proj_b",
    )(h3, g_sh, g_pre, w_k, w_vt, w_f, b_f, w_qt, w_qm, gsel, *cast_params)


def _fox_kernel(k_ref, qt_ref, vt_ref, o_ref, s0_ref, s1_ref, bm0_ref, bm1_ref,
                p0_ref, p1_ref, a0_ref, a1_ref, bias_ref, m_ref, acc_ref):
    tk = FOX_TK
    tq = FOX_TQ
    n_q = k_ref.shape[2] // tq
    assert tq == tk and n_q % STEPS_PER_ITER == 0
    sub = tq // tk
    s_refs = (s0_ref, s1_ref)
    bm_refs = (bm0_ref, bm1_ref)
    p_refs = (p0_ref, p1_ref)
    a_refs = (a0_ref, a1_ref)

    def issue_scores(par, qi, kb, with_max):
        k0 = pl.multiple_of(kb * tk, tk)
        q0 = pl.multiple_of(qi * tq, tq)
        for hh in range(2):
            s = _dot(k_ref[0, hh, pl.ds(k0, tk), :],
                     qt_ref[0, hh * LANES:(hh + 1) * LANES, pl.ds(q0, tq)])
            s_refs[par][hh] = s
            if with_max:
                bm_refs[par][hh] = jnp.max(s, axis=0, keepdims=True)

    def accumulate(par, qi, kb):
        k0 = pl.multiple_of(kb * tk, tk)
        q0 = pl.multiple_of(qi * tq, tq)
        for hh in range(2):
            pv = _dot(vt_ref[0, hh, :, pl.ds(k0, tk)], p_refs[par][hh])
            acc_ref[hh, :, pl.ds(q0, tq)] = a_refs[par][hh] * acc_ref[hh, :, pl.ds(q0, tq)] + pv

    def softmax(idx, qi, mask_id):
        q0 = pl.multiple_of(qi * tq, tq)
        par = idx % 2
        for hh in range(2):
            s = s_refs[par][hh]
            if mask_id is None:
                m_blk = bm_refs[par][hh]
            else:
                s = s + bias_ref[mask_id]
                m_blk = jnp.max(s, axis=0, keepdims=True)
            m_old = m_ref[hh, :, pl.ds(q0, tq)]
            m_new = jnp.maximum(m_old, m_blk)
            m_ref[hh, :, pl.ds(q0, tq)] = m_new
            a_refs[par][hh] = jnp.exp2(m_old - m_new)
            p_refs[par][hh] = jnp.exp2(s - m_new).astype(BF16)

    def finalize(qi):
        q0 = pl.multiple_of(qi * tq, tq)
        halves = [acc_ref[hh, :HEAD_DIM, pl.ds(q0, tq)]
                  / acc_ref[hh, HEAD_DIM:HEAD_DIM + 1, pl.ds(q0, tq)] for hh in range(2)]
        o_ref[0, pl.ds(q0, tq), :] = jnp.concatenate(halves, axis=0).T.astype(BF16)

    def step(idx, prev, cur, nxt, mask_id, after_accumulate=None):
        accumulate((idx + 1) % 2, *prev)
        if after_accumulate is not None:
            after_accumulate()
        issue_scores((idx + 1) % 2, *nxt, with_max=mask_id is None)
        softmax(idx, cur[0], mask_id)

    key_pos = lax.broadcasted_iota(jnp.int32, (tk, tq), 0)
    qry_pos = lax.broadcasted_iota(jnp.int32, (tk, tq), 1)
    for d in range(sub):
        bias_ref[d] = jnp.where(key_pos + d * tk <= qry_pos, 0.0, -jnp.inf)
    m_ref[...] = jnp.full(m_ref.shape, -jnp.inf, F32)
    acc_ref[...] = jnp.zeros(acc_ref.shape, F32)
    p_refs[1][...] = jnp.zeros(p_refs[1].shape, BF16)
    a_refs[1][...] = jnp.zeros(a_refs[1].shape, F32)

    def below_next(qi, kb):
        more = kb + 1 < qi * sub
        wrap = qi + 1 < n_q
        return (jnp.where(more, qi, jnp.where(wrap, qi + 1, 0)),
                jnp.where(more, kb + 1, 0))

    issue_scores(0, 1, 0, with_max=True)

    def below_body(_, carry):
        prev, cur = carry[:2], carry[2:]
        for idx in range(STEPS_PER_ITER):
            nxt = below_next(*cur)
            step(idx, prev, cur, nxt, None)
            prev, cur = cur, nxt
        return prev + cur

    n_below = sub * n_q * (n_q - 1) // 2
    assert STEPS_PER_ITER % 2 == 0 and n_below % STEPS_PER_ITER == 0
    zero = jnp.int32(0)
    carry = lax.fori_loop(0, n_below // STEPS_PER_ITER, below_body,
                          (zero, zero, jnp.int32(1), zero))

    def diag_body(j, prev):
        q0 = j * STEPS_PER_ITER
        for idx in range(STEPS_PER_ITER):
            q = q0 + idx
            qn = jnp.minimum(q + 1, n_q - 1)
            done = (lambda qd=q - 1: finalize(qd)) if idx > 0 else None
            step(idx, prev, (q, q), (qn, qn), 0, after_accumulate=done)
            prev = (q, q)

        @pl.when(j >= 1)
        def _():
            finalize(q0 - 1)

        return prev

    prev = lax.fori_loop(0, n_q // STEPS_PER_ITER, diag_body, carry[:2])
    accumulate(1, *prev)
    finalize(n_q - 1)


def _fox_attention(kp, qt, vt):
    b, _, s, _ = kp.shape
    pairs = FOX_HEADS // 2
    return pl.pallas_call(
        _fox_kernel,
        grid=(b, pairs),
        in_specs=[
            pl.BlockSpec((1, 2, s, LANES), lambda i, p: (i, p, 0, 0)),
            pl.BlockSpec((1, 2 * LANES, s), lambda i, p: (i, p, 0)),
            pl.BlockSpec((1, 2, V_ROWS, s), lambda i, p: (i, p, 0, 0)),
        ],
        out_specs=pl.BlockSpec((1, s, LANES), lambda i, p: (i, 0, p)),
        out_shape=jax.ShapeDtypeStruct((b, s, MAIN_WIDTH), BF16),
        scratch_shapes=[
            pltpu.VMEM((2, FOX_TK, FOX_TQ), F32),
            pltpu.VMEM((2, FOX_TK, FOX_TQ), F32),
            pltpu.VMEM((2, 1, FOX_TQ), F32),
            pltpu.VMEM((2, 1, FOX_TQ), F32),
            pltpu.VMEM((2, FOX_TK, FOX_TQ), BF16),
            pltpu.VMEM((2, FOX_TK, FOX_TQ), BF16),
            pltpu.VMEM((2, 1, FOX_TQ), F32),
            pltpu.VMEM((2, 1, FOX_TQ), F32),
            pltpu.VMEM((FOX_TQ // FOX_TK, FOX_TK, FOX_TQ), F32),
            pltpu.VMEM((2, 1, s), F32),
            pltpu.VMEM((2, V_ROWS, s), F32),
        ],
        compiler_params=_params("arbitrary", "arbitrary"),
        name="fox_attention",
    )(kp, qt, vt)


def _mixer_b_kernel(h_ref, attn_ref, qm_ref, kt_ref, vm_ref, w_out_ref, g_post_ref, o_ref):
    h = h_ref[...]
    s_mem = _memory_scores(qm_ref[...], kt_ref[0, 0])
    y_main = _dot(attn_ref[...], w_out_ref[:MAIN_WIDTH, :])
    mem_o = _memory_readout(s_mem, vm_ref[0, 0]).astype(BF16)
    y = y_main + _dot(mem_o, w_out_ref[MAIN_WIDTH:, :])
    o_ref[...] = h + _rms(y, g_post_ref[...])


def _mixer_b(h, attn, qm, kt, vm, layer, w_out, g_post, seq, tm):
    t, d = h.shape
    per_batch = seq // tm
    tile = lambda w: pl.BlockSpec((tm, w), lambda i: (i, 0))
    return pl.pallas_call(
        _mixer_b_kernel,
        grid=(t // tm,),
        in_specs=[tile(d), tile(MAIN_WIDTH), tile(MEM_WIDTH),
                  pl.BlockSpec((1, 1) + kt.shape[2:], lambda i: (layer, i // per_batch, 0, 0)),
                  pl.BlockSpec((1, 1) + vm.shape[2:], lambda i: (layer, i // per_batch, 0, 0)),
                  _full(w_out.shape), _full(g_post.shape)],
        out_specs=tile(d),
        out_shape=jax.ShapeDtypeStruct((t, d), F32),
        compiler_params=_params("arbitrary"),
        name="mixer_b",
    )(h, attn, qm, kt, vm, w_out, g_post)


def kernel(x, mem, ln_mix_pre, ln_mix_post, ln_ffn_pre, ln_ffn_post, ln_mem, w_mem_kv, w_out,
           w_ffn_gate, w_ffn_up, w_ffn_down, w_in_a, w_spatial, b_spatial, ln_v_g, ln_v_b,
           ln_shared, w_shared_kv, b_forget, w_in_b):
    b, s, d = x.shape
    t = b * s
    row = lambda g: g.reshape(1, -1)
    bf = lambda w: w.astype(BF16)

    kt_mem, v_mem = _memkv(mem, ln_mem, w_mem_kv)
    h = x.reshape(t, d)

    ffn_weights = (w_ffn_gate, w_ffn_up, w_ffn_down)
    h, wg, wu, wd = _mixer_a(
        h, kt_mem, v_mem, 0, row(ln_mix_pre[0]), bf(w_in_a[0]), w_spatial[0],
        b_spatial[0].reshape(A_GROUPS, CHUNK, 1), row(ln_v_g[0]), row(ln_v_b[0]),
        bf(w_out[0]), row(ln_mix_post[0]), s, MIXER_A_TILE, ffn_weights, 0)
    h = _ffn(h, row(ln_ffn_pre[0]), wg, wu, wd, row(ln_ffn_post[0]), FFN_TILE)

    pad = LANES - FOX_HEADS
    w_f = jnp.pad(w_shared_kv[:, 2 * MAIN_WIDTH:], ((0, 0), (0, pad)))
    b_f = jnp.pad(b_forget, (0, pad)).reshape(1, LANES)
    kp, qt, vt, qm, wg, wu, wd, w_out1 = _proj_b(
        h.reshape(b, s, d), row(ln_shared), row(ln_mix_pre[1]),
        bf(w_shared_kv[:, :MAIN_WIDTH]), bf(w_shared_kv[:, MAIN_WIDTH:2 * MAIN_WIDTH].T),
        bf(w_f), b_f, bf(w_in_b[0][:, :MAIN_WIDTH].T), bf(w_in_b[0][:, MAIN_WIDTH:]), PROJ_TILE,
        ffn_weights + (w_out,), 1)
    attn = _fox_attention(kp, qt, vt)
    h = _mixer_b(h, attn.reshape(t, MAIN_WIDTH), qm.reshape(t, MEM_WIDTH), kt_mem, v_mem, 1,
                 w_out1, row(ln_mix_post[1]), s, TOKEN_TILE)
    h = _ffn(h, row(ln_ffn_pre[1]), wg, wu, wd, row(ln_ffn_post[1]), FFN_TILE)
    return h.reshape(b, s, d)
```

```python
import functools
import math

import numpy as np
import jax
import jax.numpy as jnp
from jax import lax
from jax.experimental import pallas as pl
from jax.experimental.pallas import tpu as pltpu

F32 = jnp.float32
BF16 = jnp.bfloat16

D_MODEL = 1024
HEAD_DIM = 64
MEM_HEADS = 4
MEM_WIDTH = MEM_HEADS * HEAD_DIM
MAIN_WIDTH = D_MODEL - MEM_WIDTH
CHUNK = 128
A_GROUPS = 6
FOX_HEADS = MAIN_WIDTH // HEAD_DIM
RMS_EPS = 1e-6
LN_EPS = 1e-5
SCALE = HEAD_DIM ** -0.5
LOG2E = math.log2(math.e)
LANES = 128
VMEM_LIMIT = 56 * 1024 * 1024
TOKEN_TILE = 1024
MIXER_A_TILE = 1024
MIXER_SUB_TILES = 4
PROJ_TILE = 1024
PROJ_SUB_TILES = 4
FFN_TILE = 1024
FFN_SUB_ROWS = 256

N_PIECES = 3
PIECE_STRIDE = 16
CT_ROW = 36
V_ROWS = 80
FOX_TK = 256
FOX_TQ = 512
STEPS_PER_ITER = 4

NT_DIMS = (((1,), (1,)), ((), ()))


def _dot(a, b):
    return jnp.dot(a, b, preferred_element_type=F32)


def _dot_nt(a, b):
    return lax.dot_general(a, b, NT_DIMS, preferred_element_type=F32)


def _rms(x, g):
    return (x * lax.rsqrt(jnp.mean(x * x, axis=-1, keepdims=True) + RMS_EPS)) * g


def _split_bf16(x):
    p1 = x.astype(BF16)
    r1 = x - p1.astype(F32)
    p2 = r1.astype(BF16)
    p3 = (r1 - p2.astype(F32)).astype(BF16)
    return p1, p2, p3


def _params(*sem):
    return pltpu.CompilerParams(dimension_semantics=sem, vmem_limit_bytes=VMEM_LIMIT)


def _full(shape):
    n = len(shape)
    return pl.BlockSpec(shape, lambda *_: (0,) * n)


def _cast_specs(params, layer, n_steps, flat_step):
    in_specs, out_specs, out_shapes = [], [], []
    for w in params:
        _, r, c = w.shape
        rows = r // n_steps
        assert rows * n_steps == r and rows % 16 == 0
        in_specs.append(pl.BlockSpec((1, rows, c), lambda *g: (layer, flat_step(*g), 0)))
        out_specs.append(pl.BlockSpec((rows, c), lambda *g: (flat_step(*g), 0)))
        out_shapes.append(jax.ShapeDtypeStruct((r, c), BF16))
    return in_specs, out_specs, out_shapes


def _cast_slabs(src_refs, dst_refs):
    for src, dst in zip(src_refs, dst_refs):
        dst[...] = src[0].astype(BF16)


def _memkv_kernel(mem_ref, g_ref, w_ref, kt_ref, v_ref):
    m = mem_ref.shape[1]
    n = _rms(mem_ref[0], g_ref[0])
    kv = _dot(n.astype(BF16), w_ref[0].astype(BF16))
    k_t = kv[:, :MEM_WIDTH].T
    v = kv[:, MEM_WIDTH:]
    head_of_row = lax.broadcasted_iota(jnp.int32, (MEM_WIDTH, 1), 0) // HEAD_DIM
    head_of_lane = lax.broadcasted_iota(jnp.int32, (1, MEM_WIDTH), 1) // HEAD_DIM
    for hh in range(MEM_HEADS):
        kt_ref[0, 0, :, hh * m:(hh + 1) * m] = jnp.where(head_of_row == hh, k_t, 0.0).astype(BF16)
        v_ref[0, 0, hh * m:(hh + 1) * m, :] = jnp.where(head_of_lane == hh, v, 0.0).astype(BF16)


def _memkv(mem, ln_mem, w_mem_kv):
    depth = ln_mem.shape[0]
    b, m, d = mem.shape
    n_out = w_mem_kv.shape[-1]
    return pl.pallas_call(
        _memkv_kernel,
        grid=(depth, b),
        in_specs=[
            pl.BlockSpec((1, m, d), lambda l, i: (i, 0, 0)),
            pl.BlockSpec((1, 1, d), lambda l, i: (l, 0, 0)),
            pl.BlockSpec((1, d, n_out), lambda l, i: (l, 0, 0)),
        ],
        out_specs=[
            pl.BlockSpec((1, 1, MEM_WIDTH, MEM_HEADS * m), lambda l, i: (l, i, 0, 0)),
            pl.BlockSpec((1, 1, MEM_HEADS * m, MEM_WIDTH), lambda l, i: (l, i, 0, 0)),
        ],
        out_shape=[
            jax.ShapeDtypeStruct((depth, b, MEM_WIDTH, MEM_HEADS * m), BF16),
            jax.ShapeDtypeStruct((depth, b, MEM_HEADS * m, MEM_WIDTH), BF16),
        ],
        compiler_params=_params("arbitrary", "arbitrary"),
        name="memkv",
    )(mem, ln_mem.reshape(depth, 1, d), w_mem_kv)


def _memory_scores(qm, kt):
    return _dot(qm, kt)


def _memory_readout(s_all, v):
    m = s_all.shape[1] // MEM_HEADS
    probs = []
    for hh in range(MEM_HEADS):
        s = s_all[:, hh * m:(hh + 1) * m]
        p = jnp.exp(s - jnp.max(s, axis=-1, keepdims=True))
        inv = 1.0 / jnp.sum(p, axis=-1, keepdims=True)
        probs.append((p * inv).astype(BF16))
    return _dot(jnp.concatenate(probs, axis=1), v)


def _mixer_a_kernel(n_cast, h_ref, kt_ref, vm_ref, g_pre_ref, w_in_ref, wsp_ref, bsp_ref, lng_ref,
                    lnb_ref, w_out_ref, g_post_ref, *rest):
    cast_src, (o_ref, *cast_dst), mixed_ref = rest[:n_cast], rest[n_cast:-1], rest[-1]
    _cast_slabs(cast_src, cast_dst)
    sub = h_ref.shape[0] // MIXER_SUB_TILES
    n_chunks = sub // CHUNK
    row = lax.broadcasted_iota(jnp.int32, (CHUNK, CHUNK), 0)
    col = lax.broadcasted_iota(jnp.int32, (CHUNK, CHUNK), 1)
    causal = row >= col
    w_sp = [jnp.where(causal, wsp_ref[g], 0.0).astype(BF16) for g in range(A_GROUPS)]

    def in_proj(r):
        h = h_ref[r * sub:(r + 1) * sub, :]
        a = _rms(h, g_pre_ref[...]).astype(BF16)
        return h, _dot(a, w_in_ref[...])

    def mix(r, h, proj):
        base = r * sub
        s_mem = _memory_scores((proj[:, 2 * MAIN_WIDTH:] * SCALE).astype(BF16), kt_ref[0, 0])
        u = jax.nn.gelu(proj[:, :MAIN_WIDTH])
        v = jax.nn.gelu(proj[:, MAIN_WIDTH:2 * MAIN_WIDTH])
        mu = jnp.mean(v, axis=-1, keepdims=True)
        vc = v - mu
        vn = vc * lax.rsqrt(jnp.mean(vc * vc, axis=-1, keepdims=True) + LN_EPS)
        vn = (vn * lng_ref[...] + lnb_ref[...]).astype(BF16)
        for g in range(A_GROUPS):
            bias = bsp_ref[g]
            cols = slice(g * CHUNK, (g + 1) * CHUNK)
            v_g = jnp.concatenate([vn[c * CHUNK:(c + 1) * CHUNK, cols] for c in range(n_chunks)],
                                  axis=1)
            s_g = _dot(w_sp[g], v_g)
            for c in range(n_chunks):
                rows = slice(c * CHUNK, (c + 1) * CHUNK)
                s = s_g[:, c * CHUNK:(c + 1) * CHUNK] + bias
                mixed_ref[base + c * CHUNK:base + (c + 1) * CHUNK, cols] = (
                    u[rows, cols] * s).astype(BF16)
        mixed_ref[base:base + sub, MAIN_WIDTH:] = _memory_readout(s_mem, vm_ref[0, 0]).astype(BF16)
        y = _dot(mixed_ref[base:base + sub, :], w_out_ref[...])
        o_ref[base:base + sub, :] = h + _rms(y, g_post_ref[...])

    nxt = in_proj(0)
    for r in range(MIXER_SUB_TILES):
        cur = nxt
        if r + 1 < MIXER_SUB_TILES:
            nxt = in_proj(r + 1)
        mix(r, *cur)


def _mixer_a(h, kt, vm, layer, g_pre, w_in, wsp, bsp, lng, lnb, w_out, g_post, seq, tm,
             cast_params, cast_layer):
    t, d = h.shape
    per_batch = seq // tm
    c_in, c_out, c_shape = _cast_specs(cast_params, cast_layer, t // tm, lambda i: i)
    return pl.pallas_call(
        functools.partial(_mixer_a_kernel, len(cast_params)),
        grid=(t // tm,),
        in_specs=[
            pl.BlockSpec((tm, d), lambda i: (i, 0)),
            pl.BlockSpec((1, 1) + kt.shape[2:], lambda i: (layer, i // per_batch, 0, 0)),
            pl.BlockSpec((1, 1) + vm.shape[2:], lambda i: (layer, i // per_batch, 0, 0)),
            _full(g_pre.shape), _full(w_in.shape), _full(wsp.shape), _full(bsp.shape),
            _full(lng.shape), _full(lnb.shape), _full(w_out.shape), _full(g_post.shape),
        ] + c_in,
        out_specs=[pl.BlockSpec((tm, d), lambda i: (i, 0))] + c_out,
        out_shape=[jax.ShapeDtypeStruct((t, d), F32)] + c_shape,
        scratch_shapes=[pltpu.VMEM((tm, d), BF16)],
        compiler_params=_params("arbitrary"),
        name="mixer_a",
    )(h, kt, vm, g_pre, w_in, wsp, bsp, lng, lnb, w_out, g_post, *cast_params)


def _ffn_kernel(h_ref, g_pre_ref, wg_ref, wu_ref, wd_ref, g_post_ref, o_ref):
    sub = FFN_SUB_ROWS
    for r in range(h_ref.shape[0] // sub):
        rows = slice(r * sub, (r + 1) * sub)
        h = h_ref[rows, :]
        f = _rms(h, g_pre_ref[...]).astype(BF16)
        gate = _dot(f, wg_ref[...])
        up = _dot(f, wu_ref[...])
        act = (jax.nn.silu(gate) * up).astype(BF16)
        y = _dot(act, wd_ref[...])
        o_ref[rows, :] = h + _rms(y, g_post_ref[...])


def _ffn(h, g_pre, wg, wu, wd, g_post, tm):
    t, d = h.shape
    return pl.pallas_call(
        _ffn_kernel,
        grid=(t // tm,),
        in_specs=[
            pl.BlockSpec((tm, d), lambda i: (i, 0)),
            _full(g_pre.shape), _full(wg.shape), _full(wu.shape), _full(wd.shape),
            _full(g_post.shape),
        ],
        out_specs=pl.BlockSpec((tm, d), lambda i: (i, 0)),
        out_shape=jax.ShapeDtypeStruct((t, d), F32),
        compiler_params=_params("arbitrary"),
        name="ffn",
    )(h, g_pre, wg, wu, wd, g_post)


def _gate_selector():
    g = np.zeros((LANES, LANES), np.float32)
    for i in range(N_PIECES):
        for hd in range(FOX_HEADS):
            g[PIECE_STRIDE * i + hd, HEAD_DIM + N_PIECES * hd + i] = -1.0
    return jnp.asarray(g, dtype=BF16)


def _pack_pieces(pieces):
    packed = pieces[0].astype(F32)
    for i in range(1, N_PIECES):
        packed = packed + pltpu.roll(pieces[i].astype(F32), PIECE_STRIDE * i, axis=1)
    return packed.astype(BF16)


def _proj_b_kernel(n_cast, h_ref, g_sh_ref, g_pre_ref, w_k_ref, w_vt_ref, w_f_ref, b_f_ref,
                   w_qt_ref, w_qm_ref, gsel_ref, *rest):
    cast_src, carry_ref = rest[:n_cast], rest[-1]
    kp_ref, qt_ref, vt_ref, qm_ref, *cast_dst = rest[n_cast:-1]
    _cast_slabs(cast_src, cast_dst)
    sub = h_ref.shape[1] // PROJ_SUB_TILES

    @pl.when(pl.program_id(1) == 0)
    def _():
        carry_ref[...] = jnp.zeros_like(carry_ref)

    lane = lax.broadcasted_iota(jnp.int32, (1, LANES), 1)
    row = lax.broadcasted_iota(jnp.int32, (sub, sub), 0)
    col = lax.broadcasted_iota(jnp.int32, (sub, sub), 1)
    tri = (row >= col).astype(BF16)
    low = lane < HEAD_DIM
    ones_lanes = jnp.where((lane >= HEAD_DIM + CT_ROW) & (lane < HEAD_DIM + CT_ROW + N_PIECES),
                           1.0, 0.0)
    rid = lax.broadcasted_iota(jnp.int32, (HEAD_DIM, 1), 0)
    rid_v = lax.broadcasted_iota(jnp.int32, (V_ROWS - HEAD_DIM, sub), 0)
    ones_row = jnp.where(rid_v == 0, 1.0, 0.0).astype(BF16)

    def project(r):
        rows = slice(r * sub, (r + 1) * sub)
        h = h_ref[0, rows, :]
        hn = h * lax.rsqrt(jnp.mean(h * h, axis=-1, keepdims=True) + RMS_EPS)
        s_in = (hn * g_sh_ref[...]).astype(BF16)
        a = (hn * g_pre_ref[...]).astype(BF16)
        log_f = jax.nn.log_sigmoid(_dot(s_in, w_f_ref[...]) + b_f_ref[...])
        log_f = jnp.where(lane < FOX_HEADS, log_f, 0.0)
        k = _dot(s_in, w_k_ref[...])
        qt = _dot_nt(w_qt_ref[...], a) * (SCALE * LOG2E)
        vt = _dot_nt(w_vt_ref[...], s_in)
        qm_ref[0, rows, :] = (_dot(a, w_qm_ref[...]) * SCALE).astype(BF16)
        return log_f, k, qt, vt

    def assemble(r, log_f, k, qt, vt):
        rows = slice(r * sub, (r + 1) * sub)
        sums = _dot(tri, _pack_pieces(_split_bf16(log_f)))
        c = sums + carry_ref[...]
        for i in range(1, N_PIECES):
            c = c + pltpu.roll(sums, LANES - PIECE_STRIDE * i, axis=1)
        c = jnp.where(lane < FOX_HEADS, c, 0.0)
        carry_ref[...] = c[sub - 1:sub, :]
        c2 = c * LOG2E

        aug = _dot(_pack_pieces(_split_bf16(c2)), gsel_ref[...]) + ones_lanes
        for p in range(FOX_HEADS // 2):
            blk = k[:, p * LANES:(p + 1) * LANES]
            kp_ref[0, 2 * p, rows, :] = jnp.where(low, blk, aug).astype(BF16)
            kp_ref[0, 2 * p + 1, rows, :] = jnp.where(
                low, pltpu.roll(blk, HEAD_DIM, axis=1), aug).astype(BF16)

        t1, t2, t3 = _split_bf16(c2.T)
        for hd in range(FOX_HEADS):
            sel = jnp.where((rid >= N_PIECES * hd) & (rid < N_PIECES * (hd + 1)), 1.0, 0.0)
            aug_q = jnp.where(rid == CT_ROW, t1[hd:hd + 1, :].astype(F32),
                              jnp.where(rid == CT_ROW + 1, t2[hd:hd + 1, :].astype(F32),
                                        jnp.where(rid == CT_ROW + 2, t3[hd:hd + 1, :].astype(F32),
                                                  sel)))
            qt_ref[0, hd * LANES:hd * LANES + HEAD_DIM, rows] = (
                qt[hd * HEAD_DIM:(hd + 1) * HEAD_DIM, :].astype(BF16))
            qt_ref[0, hd * LANES + HEAD_DIM:(hd + 1) * LANES, rows] = aug_q.astype(BF16)
            vt_ref[0, hd, :HEAD_DIM, rows] = vt[hd * HEAD_DIM:(hd + 1) * HEAD_DIM, :].astype(BF16)
            vt_ref[0, hd, HEAD_DIM:, rows] = ones_row

    nxt = project(0)
    for r in range(PROJ_SUB_TILES):
        cur = nxt
        if r + 1 < PROJ_SUB_TILES:
            nxt = project(r + 1)
        assemble(r, *cur)


def _proj_b(h3, g_sh, g_pre, w_k, w_vt, w_f, b_f, w_qt, w_qm, tm, cast_params, cast_layer):
    b, s, d = h3.shape
    gsel = _gate_selector()
    per_batch = s // tm
    c_in, c_out, c_shape = _cast_specs(cast_params, cast_layer, b * per_batch,
                                       lambda i, j: i * per_batch + j)
    return pl.pallas_call(
        functools.partial(_proj_b_kernel, len(cast_params)),
        grid=(b, per_batch),
        in_specs=[pl.BlockSpec((1, tm, d), lambda i, j: (i, j, 0)),
                  _full(g_sh.shape), _full(g_pre.shape), _full(w_k.shape), _full(w_vt.shape),
                  _full(w_f.shape), _full(b_f.shape), _full(w_qt.shape), _full(w_qm.shape),
                  _full(gsel.shape)] + c_in,
        out_specs=[
            pl.BlockSpec((1, FOX_HEADS, tm, LANES), lambda i, j: (i, 0, j, 0)),
            pl.BlockSpec((1, FOX_HEADS * LANES, tm), lambda i, j: (i, 0, j)),
            pl.BlockSpec((1, FOX_HEADS, V_ROWS, tm), lambda i, j: (i, 0, 0, j)),
            pl.BlockSpec((1, tm, MEM_WIDTH), lambda i, j: (i, j, 0)),
        ] + c_out,
        out_shape=[
            jax.ShapeDtypeStruct((b, FOX_HEADS, s, LANES), BF16),
            jax.ShapeDtypeStruct((b, FOX_HEADS * LANES, s), BF16),
            jax.ShapeDtypeStruct((b, FOX_HEADS, V_ROWS, s), BF16),
            jax.ShapeDtypeStruct((b, s, MEM_WIDTH), BF16),
        ] + c_shape,
        scratch_shapes=[pltpu.VMEM((1, LANES), F32)],
        compiler_params=_params("arbitrary", "arbitrary"),
        name="proj_b",
    )(h3, g_sh, g_pre, w_k, w_vt, w_f, b_f, w_qt, w_qm, gsel, *cast_params)


def _fox_kernel(k_ref, qt_ref, vt_ref, o_ref, s0_ref, s1_ref, bm0_ref, bm1_ref,
                p0_ref, p1_ref, a0_ref, a1_ref, bias_ref, m_ref, acc_ref):
    tk = FOX_TK
    tq = FOX_TQ
    n_q = k_ref.shape[2] // tq
    assert tq == 2 * tk and n_q >= 2 and n_q % 2 == 0
    sub = 2
    s_refs = (s0_ref, s1_ref)
    bm_refs = (bm0_ref, bm1_ref)
    p_refs = (p0_ref, p1_ref)
    a_refs = (a0_ref, a1_ref)

    def issue_scores(par, qi, kb, with_max):
        k0 = pl.multiple_of(kb * tk, tk)
        q0 = pl.multiple_of(qi * tq, tq)
        for hh in range(2):
            s = _dot(k_ref[0, hh, pl.ds(k0, tk), :],
                     qt_ref[0, hh * LANES:(hh + 1) * LANES, pl.ds(q0, tq)])
            s_refs[par][hh] = s
            if with_max:
                bm_refs[par][hh] = jnp.max(s, axis=0, keepdims=True)

    def accumulate(par, qi, kb):
        k0 = pl.multiple_of(kb * tk, tk)
        q0 = pl.multiple_of(qi * tq, tq)
        for hh in range(2):
            pv = _dot(vt_ref[0, hh, :, pl.ds(k0, tk)], p_refs[par][hh])
            acc_ref[hh, :, pl.ds(q0, tq)] = a_refs[par][hh] * acc_ref[hh, :, pl.ds(q0, tq)] + pv

    def softmax(idx, qi, mask_id):
        q0 = pl.multiple_of(qi * tq, tq)
        par = idx % 2
        for hh in range(2):
            s = s_refs[par][hh]
            if mask_id is None:
                m_blk = bm_refs[par][hh]
            else:
                s = s + bias_ref[mask_id]
                m_blk = jnp.max(s, axis=0, keepdims=True)
            m_old = m_ref[hh, :, pl.ds(q0, tq)]
            m_new = jnp.maximum(m_old, m_blk)
            m_ref[hh, :, pl.ds(q0, tq)] = m_new
            a_refs[par][hh] = jnp.exp2(m_old - m_new)
            p_refs[par][hh] = jnp.exp2(s - m_new).astype(BF16)

    def finalize(qi):
        q0 = pl.multiple_of(qi * tq, tq)
        halves = [acc_ref[hh, :HEAD_DIM, pl.ds(q0, tq)]
                  / acc_ref[hh, HEAD_DIM:HEAD_DIM + 1, pl.ds(q0, tq)] for hh in range(2)]
        o_ref[0, pl.ds(q0, tq), :] = jnp.concatenate(halves, axis=0).T.astype(BF16)

    def step(idx, prev, cur, nxt, mask_id, after_accumulate=None):
        accumulate((idx + 1) % 2, *prev)
        if after_accumulate is not None:
            after_accumulate()
        issue_scores((idx + 1) % 2, *nxt, with_max=mask_id is None)
        softmax(idx, cur[0], mask_id)

    key_pos = lax.broadcasted_iota(jnp.int32, (tk, tq), 0)
    qry_pos = lax.broadcasted_iota(jnp.int32, (tk, tq), 1)
    for d in range(sub):
        bias_ref[d] = jnp.where(key_pos + d * tk <= qry_pos, 0.0, -jnp.inf)
    m_ref[...] = jnp.full(m_ref.shape, -jnp.inf, F32)
    acc_ref[...] = jnp.zeros(acc_ref.shape, F32)
    p_refs[1][...] = jnp.zeros(p_refs[1].shape, BF16)
    a_refs[1][...] = jnp.zeros(a_refs[1].shape, F32)

    def below_next(qi, kb):
        more = kb + 1 < qi * sub
        wrap = qi + 1 < n_q
        return (jnp.where(more, qi, jnp.where(wrap, qi + 1, 0)),
                jnp.where(more, kb + 1, 0))

    issue_scores(0, 1, 0, with_max=True)

    def below_body(_, carry):
        prev, cur = carry[:2], carry[2:]
        for idx in range(STEPS_PER_ITER):
            nxt = below_next(*cur)
            step(idx, prev, cur, nxt, None)
            prev, cur = cur, nxt
        return prev + cur

    n_below = sub * n_q * (n_q - 1) // 2
    assert STEPS_PER_ITER == 2 * sub and n_below % STEPS_PER_ITER == 0
    zero = jnp.int32(0)
    carry = lax.fori_loop(0, n_below // STEPS_PER_ITER, below_body,
                          (zero, zero, jnp.int32(1), zero))

    def diag_body(j, prev):
        qa = 2 * j
        qb = qa + 1
        ka = qa * sub
        kb = qb * sub
        step(0, prev, (qa, ka), (qa, ka + 1), 0)
        step(1, (qa, ka), (qa, ka + 1), (qb, kb), 1)
        step(2, (qa, ka + 1), (qb, kb), (qb, kb + 1), 0, after_accumulate=lambda: finalize(qa))
        qn = jnp.minimum(qb + 1, n_q - 1)
        step(3, (qb, kb), (qb, kb + 1), (qn, qn * sub), 1)

        @pl.when(j >= 1)
        def _():
            finalize(qa - 1)

        return (qb, kb + 1)

    prev = lax.fori_loop(0, n_q // 2, diag_body, carry[:2])
    accumulate(1, *prev)
    finalize(n_q - 1)


def _fox_attention(kp, qt, vt):
    b, _, s, _ = kp.shape
    pairs = FOX_HEADS // 2
    return pl.pallas_call(
        _fox_kernel,
        grid=(b, pairs),
        in_specs=[
            pl.BlockSpec((1, 2, s, LANES), lambda i, p: (i, p, 0, 0)),
            pl.BlockSpec((1, 2 * LANES, s), lambda i, p: (i, p, 0)),
            pl.BlockSpec((1, 2, V_ROWS, s), lambda i, p: (i, p, 0, 0)),
        ],
        out_specs=pl.BlockSpec((1, s, LANES), lambda i, p: (i, 0, p)),
        out_shape=jax.ShapeDtypeStruct((b, s, MAIN_WIDTH), BF16),
        scratch_shapes=[
            pltpu.VMEM((2, FOX_TK, FOX_TQ), F32),
            pltpu.VMEM((2, FOX_TK, FOX_TQ), F32),
            pltpu.VMEM((2, 1, FOX_TQ), F32),
            pltpu.VMEM((2, 1, FOX_TQ), F32),
            pltpu.VMEM((2, FOX_TK, FOX_TQ), BF16),
            pltpu.VMEM((2, FOX_TK, FOX_TQ), BF16),
            pltpu.VMEM((2, 1, FOX_TQ), F32),
            pltpu.VMEM((2, 1, FOX_TQ), F32),
            pltpu.VMEM((FOX_TQ // FOX_TK, FOX_TK, FOX_TQ), F32),
            pltpu.VMEM((2, 1, s), F32),
            pltpu.VMEM((2, V_ROWS, s), F32),
        ],
        compiler_params=_params("arbitrary", "arbitrary"),
        name="fox_attention",
    )(kp, qt, vt)


def _mixer_b_kernel(h_ref, attn_ref, qm_ref, kt_ref, vm_ref, w_out_ref, g_post_ref, o_ref):
    h = h_ref[...]
    s_mem = _memory_scores(qm_ref[...], kt_ref[0, 0])
    y_main = _dot(attn_ref[...], w_out_ref[:MAIN_WIDTH, :])
    mem_o = _memory_readout(s_mem, vm_ref[0, 0]).astype(BF16)
    y = y_main + _dot(mem_o, w_out_ref[MAIN_WIDTH:, :])
    o_ref[...] = h + _rms(y, g_post_ref[...])


def _mixer_b(h, attn, qm, kt, vm, layer, w_out, g_post, seq, tm):
    t, d = h.shape
    per_batch = seq // tm
    tile = lambda w: pl.BlockSpec((tm, w), lambda i: (i, 0))
    return pl.pallas_call(
        _mixer_b_kernel,
        grid=(t // tm,),
        in_specs=[tile(d), tile(MAIN_WIDTH), tile(MEM_WIDTH),
                  pl.BlockSpec((1, 1) + kt.shape[2:], lambda i: (layer, i // per_batch, 0, 0)),
                  pl.BlockSpec((1, 1) + vm.shape[2:], lambda i: (layer, i // per_batch, 0, 0)),
                  _full(w_out.shape), _full(g_post.shape)],
        out_specs=tile(d),
        out_shape=jax.ShapeDtypeStruct((t, d), F32),
        compiler_params=_params("arbitrary"),
        name="mixer_b",
    )(h, attn, qm, kt, vm, w_out, g_post)


def kernel(x, mem, ln_mix_pre, ln_mix_post, ln_ffn_pre, ln_ffn_post, ln_mem, w_mem_kv, w_out,
           w_ffn_gate, w_ffn_up, w_ffn_down, w_in_a, w_spatial, b_spatial, ln_v_g, ln_v_b,
           ln_shared, w_shared_kv, b_forget, w_in_b):
    b, s, d = x.shape
    t = b * s
    row = lambda g: g.reshape(1, -1)
    bf = lambda w: w.astype(BF16)

    kt_mem, v_mem = _memkv(mem, ln_mem, w_mem_kv)
    h = x.reshape(t, d)

    ffn_weights = (w_ffn_gate, w_ffn_up, w_ffn_down)
    h, wg, wu, wd = _mixer_a(
        h, kt_mem, v_mem, 0, row(ln_mix_pre[0]), bf(w_in_a[0]), w_spatial[0],
        b_spatial[0].reshape(A_GROUPS, CHUNK, 1), row(ln_v_g[0]), row(ln_v_b[0]),
        bf(w_out[0]), row(ln_mix_post[0]), s, MIXER_A_TILE, ffn_weights, 0)
    h = _ffn(h, row(ln_ffn_pre[0]), wg, wu, wd, row(ln_ffn_post[0]), FFN_TILE)

    pad = LANES - FOX_HEADS
    w_f = jnp.pad(w_shared_kv[:, 2 * MAIN_WIDTH:], ((0, 0), (0, pad)))
    b_f = jnp.pad(b_forget, (0, pad)).reshape(1, LANES)
    kp, qt, vt, qm, wg, wu, wd, w_out1 = _proj_b(
        h.reshape(b, s, d), row(ln_shared), row(ln_mix_pre[1]),
        bf(w_shared_kv[:, :MAIN_WIDTH]), bf(w_shared_kv[:, MAIN_WIDTH:2 * MAIN_WIDTH].T),
        bf(w_f), b_f, bf(w_in_b[0][:, :MAIN_WIDTH].T), bf(w_in_b[0][:, MAIN_WIDTH:]), PROJ_TILE,
        ffn_weights + (w_out,), 1)
    attn = _fox_attention(kp, qt, vt)
    h = _mixer_b(h, attn.reshape(t, MAIN_WIDTH), qm.reshape(t, MEM_WIDTH), kt_mem, v_mem, 1,
                 w_out1, row(ln_mix_post[1]), s, TOKEN_TILE)
    h = _ffn(h, row(ln_ffn_pre[1]), wg, wu, wd, row(ln_ffn_post[1]), FFN_TILE)
    return h.reshape(b, s, d)
```

```python
import functools
import math

import numpy as np
import jax
import jax.numpy as jnp
from jax import lax
from jax.experimental import pallas as pl
from jax.experimental.pallas import tpu as pltpu

F32 = jnp.float32
BF16 = jnp.bfloat16

D_MODEL = 1024
HEAD_DIM = 64
MEM_HEADS = 4
MEM_WIDTH = MEM_HEADS * HEAD_DIM
MAIN_WIDTH = D_MODEL - MEM_WIDTH
CHUNK = 128
A_GROUPS = 6
FOX_HEADS = MAIN_WIDTH // HEAD_DIM
RMS_EPS = 1e-6
LN_EPS = 1e-5
SCALE = HEAD_DIM ** -0.5
LOG2E = math.log2(math.e)
LANES = 128
VMEM_LIMIT = 56 * 1024 * 1024
TOKEN_TILE = 1024
MIXER_A_TILE = 1024
MIXER_SUB_TILES = 4
PROJ_TILE = 1024
PROJ_SUB_TILES = 4
FFN_TILE = 1024
FFN_SUB_ROWS = 256

N_PIECES = 3
PIECE_STRIDE = 16
CT_ROW = 36
V_ROWS = 80
FOX_TK = 256
FOX_TQ = 512
STEPS_PER_ITER = 4

NT_DIMS = (((1,), (1,)), ((), ()))


def _dot(a, b):
    return jnp.dot(a, b, preferred_element_type=F32)


def _dot_nt(a, b):
    return lax.dot_general(a, b, NT_DIMS, preferred_element_type=F32)


def _rms(x, g):
    return (x * lax.rsqrt(jnp.mean(x * x, axis=-1, keepdims=True) + RMS_EPS)) * g


def _split_bf16(x):
    p1 = x.astype(BF16)
    r1 = x - p1.astype(F32)
    p2 = r1.astype(BF16)
    p3 = (r1 - p2.astype(F32)).astype(BF16)
    return p1, p2, p3


def _params(*sem):
    return pltpu.CompilerParams(dimension_semantics=sem, vmem_limit_bytes=VMEM_LIMIT)


def _full(shape):
    n = len(shape)
    return pl.BlockSpec(shape, lambda *_: (0,) * n)


def _cast_specs(items, layer, n_steps, flat_step):
    in_specs, out_specs, out_shapes, arrays, transposed = [], [], [], [], []
    for item in items:
        w, lyr, cols, tr = item if isinstance(item, tuple) else (item, layer, None, False)
        r = w.shape[-2]
        col0, width = cols if cols is not None else (0, w.shape[-1])
        assert col0 % width == 0 and width % LANES == 0
        cb = col0 // width
        if tr:
            rows = LANES
            rep = rows * n_steps // r
            assert rep * r == rows * n_steps
            slab = lambda *g, rep=rep: flat_step(*g) // rep
            out_specs.append(pl.BlockSpec((width, rows), lambda *g, slab=slab: (0, slab(*g))))
            out_shapes.append(jax.ShapeDtypeStruct((width, r), BF16))
        else:
            rows = r // n_steps
            assert rows * n_steps == r and rows % 16 == 0
            slab = flat_step
            out_specs.append(pl.BlockSpec((rows, width), lambda *g, slab=slab: (slab(*g), 0)))
            out_shapes.append(jax.ShapeDtypeStruct((r, width), BF16))
        if w.ndim == 3:
            in_specs.append(pl.BlockSpec(
                (1, rows, width), lambda *g, slab=slab, lyr=lyr, cb=cb: (lyr, slab(*g), cb)))
        else:
            in_specs.append(pl.BlockSpec(
                (rows, width), lambda *g, slab=slab, cb=cb: (slab(*g), cb)))
        arrays.append(w)
        transposed.append(tr)
    return in_specs, out_specs, out_shapes, arrays, tuple(transposed)


def _cast_slabs(src_refs, dst_refs, transposed=None):
    for i, (src, dst) in enumerate(zip(src_refs, dst_refs)):
        x = src[0] if len(src.shape) == 3 else src[...]
        if transposed is not None and transposed[i]:
            x = x.T
        dst[...] = x.astype(BF16)


def _memkv_kernel(mem_ref, g_ref, w_ref, kt_ref, v_ref):
    m = mem_ref.shape[1]
    n = _rms(mem_ref[0], g_ref[0])
    kv = _dot(n.astype(BF16), w_ref[0].astype(BF16))
    k_t = kv[:, :MEM_WIDTH].T
    v = kv[:, MEM_WIDTH:]
    head_of_row = lax.broadcasted_iota(jnp.int32, (MEM_WIDTH, 1), 0) // HEAD_DIM
    head_of_lane = lax.broadcasted_iota(jnp.int32, (1, MEM_WIDTH), 1) // HEAD_DIM
    for hh in range(MEM_HEADS):
        kt_ref[0, 0, :, hh * m:(hh + 1) * m] = jnp.where(head_of_row == hh, k_t, 0.0).astype(BF16)
        v_ref[0, 0, hh * m:(hh + 1) * m, :] = jnp.where(head_of_lane == hh, v, 0.0).astype(BF16)


def _memkv(mem, ln_mem, w_mem_kv):
    depth = ln_mem.shape[0]
    b, m, d = mem.shape
    n_out = w_mem_kv.shape[-1]
    return pl.pallas_call(
        _memkv_kernel,
        grid=(depth, b),
        in_specs=[
            pl.BlockSpec((1, m, d), lambda l, i: (i, 0, 0)),
            pl.BlockSpec((1, 1, d), lambda l, i: (l, 0, 0)),
            pl.BlockSpec((1, d, n_out), lambda l, i: (l, 0, 0)),
        ],
        out_specs=[
            pl.BlockSpec((1, 1, MEM_WIDTH, MEM_HEADS * m), lambda l, i: (l, i, 0, 0)),
            pl.BlockSpec((1, 1, MEM_HEADS * m, MEM_WIDTH), lambda l, i: (l, i, 0, 0)),
        ],
        out_shape=[
            jax.ShapeDtypeStruct((depth, b, MEM_WIDTH, MEM_HEADS * m), BF16),
            jax.ShapeDtypeStruct((depth, b, MEM_HEADS * m, MEM_WIDTH), BF16),
        ],
        compiler_params=_params("arbitrary", "arbitrary"),
        name="memkv",
    )(mem, ln_mem.reshape(depth, 1, d), w_mem_kv)


def _memory_scores(qm, kt):
    return _dot(qm, kt)


def _memory_readout(s_all, v):
    m = s_all.shape[1] // MEM_HEADS
    probs = []
    for hh in range(MEM_HEADS):
        s = s_all[:, hh * m:(hh + 1) * m]
        p = jnp.exp(s - jnp.max(s, axis=-1, keepdims=True))
        inv = 1.0 / jnp.sum(p, axis=-1, keepdims=True)
        probs.append((p * inv).astype(BF16))
    return _dot(jnp.concatenate(probs, axis=1), v)


def _mixer_a_kernel(n_cast, cast_t, h_ref, kt_ref, vm_ref, g_pre_ref, w_in_ref, wsp_ref, bsp_ref, lng_ref,
                    lnb_ref, w_out_ref, g_post_ref, *rest):
    cast_src, (o_ref, *cast_dst), mixed_ref = rest[:n_cast], rest[n_cast:-1], rest[-1]
    _cast_slabs(cast_src, cast_dst, cast_t)
    sub = h_ref.shape[0] // MIXER_SUB_TILES
    n_chunks = sub // CHUNK
    row = lax.broadcasted_iota(jnp.int32, (CHUNK, CHUNK), 0)
    col = lax.broadcasted_iota(jnp.int32, (CHUNK, CHUNK), 1)
    causal = row >= col
    w_sp = [jnp.where(causal, wsp_ref[g], 0.0).astype(BF16) for g in range(A_GROUPS)]

    def in_proj(r):
        h = h_ref[r * sub:(r + 1) * sub, :]
        a = _rms(h, g_pre_ref[...]).astype(BF16)
        return h, _dot(a, w_in_ref[...])

    def mix(r, h, proj):
        base = r * sub
        s_mem = _memory_scores((proj[:, 2 * MAIN_WIDTH:] * SCALE).astype(BF16), kt_ref[0, 0])
        u = jax.nn.gelu(proj[:, :MAIN_WIDTH])
        v = jax.nn.gelu(proj[:, MAIN_WIDTH:2 * MAIN_WIDTH])
        mu = jnp.mean(v, axis=-1, keepdims=True)
        vc = v - mu
        vn = vc * lax.rsqrt(jnp.mean(vc * vc, axis=-1, keepdims=True) + LN_EPS)
        vn = (vn * lng_ref[...] + lnb_ref[...]).astype(BF16)
        for g in range(A_GROUPS):
            bias = bsp_ref[g]
            cols = slice(g * CHUNK, (g + 1) * CHUNK)
            v_g = jnp.concatenate([vn[c * CHUNK:(c + 1) * CHUNK, cols] for c in range(n_chunks)],
                                  axis=1)
            s_g = _dot(w_sp[g], v_g)
            for c in range(n_chunks):
                rows = slice(c * CHUNK, (c + 1) * CHUNK)
                s = s_g[:, c * CHUNK:(c + 1) * CHUNK] + bias
                mixed_ref[base + c * CHUNK:base + (c + 1) * CHUNK, cols] = (
                    u[rows, cols] * s).astype(BF16)
        mixed_ref[base:base + sub, MAIN_WIDTH:] = _memory_readout(s_mem, vm_ref[0, 0]).astype(BF16)
        y = _dot(mixed_ref[base:base + sub, :], w_out_ref[...])
        o_ref[base:base + sub, :] = h + _rms(y, g_post_ref[...])

    nxt = in_proj(0)
    for r in range(MIXER_SUB_TILES):
        cur = nxt
        if r + 1 < MIXER_SUB_TILES:
            nxt = in_proj(r + 1)
        mix(r, *cur)


def _mixer_a(h, kt, vm, layer, g_pre, w_in, wsp, bsp, lng, lnb, w_out, g_post, seq, tm,
             cast_params, cast_layer):
    t, d = h.shape
    per_batch = seq // tm
    c_in, c_out, c_shape, c_arr, c_t = _cast_specs(cast_params, cast_layer, t // tm, lambda i: i)
    return pl.pallas_call(
        functools.partial(_mixer_a_kernel, len(c_arr), c_t),
        grid=(t // tm,),
        in_specs=[
            pl.BlockSpec((tm, d), lambda i: (i, 0)),
            pl.BlockSpec((1, 1) + kt.shape[2:], lambda i: (layer, i // per_batch, 0, 0)),
            pl.BlockSpec((1, 1) + vm.shape[2:], lambda i: (layer, i // per_batch, 0, 0)),
            _full(g_pre.shape), _full(w_in.shape), _full(wsp.shape), _full(bsp.shape),
            _full(lng.shape), _full(lnb.shape), _full(w_out.shape), _full(g_post.shape),
        ] + c_in,
        out_specs=[pl.BlockSpec((tm, d), lambda i: (i, 0))] + c_out,
        out_shape=[jax.ShapeDtypeStruct((t, d), F32)] + c_shape,
        scratch_shapes=[pltpu.VMEM((tm, d), BF16)],
        compiler_params=_params("arbitrary"),
        name="mixer_a",
    )(h, kt, vm, g_pre, w_in, wsp, bsp, lng, lnb, w_out, g_post, *c_arr)


def _ffn_kernel(h_ref, g_pre_ref, wg_ref, wu_ref, wd_ref, g_post_ref, o_ref):
    sub = FFN_SUB_ROWS
    for r in range(h_ref.shape[0] // sub):
        rows = slice(r * sub, (r + 1) * sub)
        h = h_ref[rows, :]
        f = _rms(h, g_pre_ref[...]).astype(BF16)
        gate = _dot(f, wg_ref[...])
        up = _dot(f, wu_ref[...])
        act = (jax.nn.silu(gate) * up).astype(BF16)
        y = _dot(act, wd_ref[...])
        o_ref[rows, :] = h + _rms(y, g_post_ref[...])


def _ffn(h, g_pre, wg, wu, wd, g_post, tm):
    t, d = h.shape
    return pl.pallas_call(
        _ffn_kernel,
        grid=(t // tm,),
        in_specs=[
            pl.BlockSpec((tm, d), lambda i: (i, 0)),
            _full(g_pre.shape), _full(wg.shape), _full(wu.shape), _full(wd.shape),
            _full(g_post.shape),
        ],
        out_specs=pl.BlockSpec((tm, d), lambda i: (i, 0)),
        out_shape=jax.ShapeDtypeStruct((t, d), F32),
        compiler_params=_params("arbitrary"),
        name="ffn",
    )(h, g_pre, wg, wu, wd, g_post)


def _gate_selector():
    g = np.zeros((LANES, LANES), np.float32)
    for i in range(N_PIECES):
        for hd in range(FOX_HEADS):
            g[PIECE_STRIDE * i + hd, HEAD_DIM + N_PIECES * hd + i] = -1.0
    return jnp.asarray(g, dtype=BF16)


def _pack_pieces(pieces):
    packed = pieces[0].astype(F32)
    for i in range(1, N_PIECES):
        packed = packed + pltpu.roll(pieces[i].astype(F32), PIECE_STRIDE * i, axis=1)
    return packed.astype(BF16)


def _proj_b_kernel(n_cast, h_ref, g_sh_ref, g_pre_ref, w_k_ref, w_vt_ref, w_f_ref, b_f_ref,
                   w_qt_ref, w_qm_ref, gsel_ref, *rest):
    cast_src, carry_ref = rest[:n_cast], rest[-1]
    kp_ref, qt_ref, vt_ref, qm_ref, *cast_dst = rest[n_cast:-1]
    _cast_slabs(cast_src, cast_dst)
    sub = h_ref.shape[1] // PROJ_SUB_TILES

    @pl.when(pl.program_id(1) == 0)
    def _():
        carry_ref[...] = jnp.zeros_like(carry_ref)

    lane = lax.broadcasted_iota(jnp.int32, (1, LANES), 1)
    row = lax.broadcasted_iota(jnp.int32, (sub, sub), 0)
    col = lax.broadcasted_iota(jnp.int32, (sub, sub), 1)
    tri = (row >= col).astype(BF16)
    low = lane < HEAD_DIM
    ones_lanes = jnp.where((lane >= HEAD_DIM + CT_ROW) & (lane < HEAD_DIM + CT_ROW + N_PIECES),
                           1.0, 0.0)
    rid = lax.broadcasted_iota(jnp.int32, (HEAD_DIM, 1), 0)
    rid_v = lax.broadcasted_iota(jnp.int32, (V_ROWS - HEAD_DIM, sub), 0)
    ones_row = jnp.where(rid_v == 0, 1.0, 0.0).astype(BF16)

    def project(r):
        rows = slice(r * sub, (r + 1) * sub)
        h = h_ref[0, rows, :]
        hn = h * lax.rsqrt(jnp.mean(h * h, axis=-1, keepdims=True) + RMS_EPS)
        s_in = (hn * g_sh_ref[...]).astype(BF16)
        a = (hn * g_pre_ref[...]).astype(BF16)
        log_f = jax.nn.log_sigmoid(_dot(s_in, w_f_ref[...]) + b_f_ref[...])
        log_f = jnp.where(lane < FOX_HEADS, log_f, 0.0)
        k = _dot(s_in, w_k_ref[...])
        qt = _dot_nt(w_qt_ref[...], a) * (SCALE * LOG2E)
        vt = _dot_nt(w_vt_ref[...], s_in)
        qm_ref[0, rows, :] = (_dot(a, w_qm_ref[...]) * SCALE).astype(BF16)
        return log_f, k, qt, vt

    def assemble(r, log_f, k, qt, vt):
        rows = slice(r * sub, (r + 1) * sub)
        sums = _dot(tri, _pack_pieces(_split_bf16(log_f)))
        c = sums + carry_ref[...]
        for i in range(1, N_PIECES):
            c = c + pltpu.roll(sums, LANES - PIECE_STRIDE * i, axis=1)
        c = jnp.where(lane < FOX_HEADS, c, 0.0)
        carry_ref[...] = c[sub - 1:sub, :]
        c2 = c * LOG2E

        aug = _dot(_pack_pieces(_split_bf16(c2)), gsel_ref[...]) + ones_lanes
        for p in range(FOX_HEADS // 2):
            blk = k[:, p * LANES:(p + 1) * LANES]
            kp_ref[0, 2 * p, rows, :] = jnp.where(low, blk, aug).astype(BF16)
            kp_ref[0, 2 * p + 1, rows, :] = jnp.where(
                low, pltpu.roll(blk, HEAD_DIM, axis=1), aug).astype(BF16)

        t1, t2, t3 = _split_bf16(c2.T)
        for hd in range(FOX_HEADS):
            sel = jnp.where((rid >= N_PIECES * hd) & (rid < N_PIECES * (hd + 1)), 1.0, 0.0)
            aug_q = jnp.where(rid == CT_ROW, t1[hd:hd + 1, :].astype(F32),
                              jnp.where(rid == CT_ROW + 1, t2[hd:hd + 1, :].astype(F32),
                                        jnp.where(rid == CT_ROW + 2, t3[hd:hd + 1, :].astype(F32),
                                                  sel)))
            qt_ref[0, hd * LANES:hd * LANES + HEAD_DIM, rows] = (
                qt[hd * HEAD_DIM:(hd + 1) * HEAD_DIM, :].astype(BF16))
            qt_ref[0, hd * LANES + HEAD_DIM:(hd + 1) * LANES, rows] = aug_q.astype(BF16)
            vt_ref[0, hd, :HEAD_DIM, rows] = vt[hd * HEAD_DIM:(hd + 1) * HEAD_DIM, :].astype(BF16)
            vt_ref[0, hd, HEAD_DIM:, rows] = ones_row

    nxt = project(0)
    for r in range(PROJ_SUB_TILES):
        cur = nxt
        if r + 1 < PROJ_SUB_TILES:
            nxt = project(r + 1)
        assemble(r, *cur)


def _proj_b(h3, g_sh, g_pre, w_k, w_vt, w_f, b_f, w_qt, w_qm, tm, cast_params, cast_layer):
    b, s, d = h3.shape
    gsel = _gate_selector()
    per_batch = s // tm
    c_in, c_out, c_shape, c_arr, _ = _cast_specs(cast_params, cast_layer, b * per_batch,
                                                 lambda i, j: i * per_batch + j)
    return pl.pallas_call(
        functools.partial(_proj_b_kernel, len(c_arr)),
        grid=(b, per_batch),
        in_specs=[pl.BlockSpec((1, tm, d), lambda i, j: (i, j, 0)),
                  _full(g_sh.shape), _full(g_pre.shape), _full(w_k.shape), _full(w_vt.shape),
                  _full(w_f.shape), _full(b_f.shape), _full(w_qt.shape), _full(w_qm.shape),
                  _full(gsel.shape)] + c_in,
        out_specs=[
            pl.BlockSpec((1, FOX_HEADS, tm, LANES), lambda i, j: (i, 0, j, 0)),
            pl.BlockSpec((1, FOX_HEADS * LANES, tm), lambda i, j: (i, 0, j)),
            pl.BlockSpec((1, FOX_HEADS, V_ROWS, tm), lambda i, j: (i, 0, 0, j)),
            pl.BlockSpec((1, tm, MEM_WIDTH), lambda i, j: (i, j, 0)),
        ] + c_out,
        out_shape=[
            jax.ShapeDtypeStruct((b, FOX_HEADS, s, LANES), BF16),
            jax.ShapeDtypeStruct((b, FOX_HEADS * LANES, s), BF16),
            jax.ShapeDtypeStruct((b, FOX_HEADS, V_ROWS, s), BF16),
            jax.ShapeDtypeStruct((b, s, MEM_WIDTH), BF16),
        ] + c_shape,
        scratch_shapes=[pltpu.VMEM((1, LANES), F32)],
        compiler_params=_params("arbitrary", "arbitrary"),
        name="proj_b",
    )(h3, g_sh, g_pre, w_k, w_vt, w_f, b_f, w_qt, w_qm, gsel, *c_arr)


def _fox_kernel(k_ref, qt_ref, vt_ref, o_ref, s0_ref, s1_ref, bm0_ref, bm1_ref,
                p0_ref, p1_ref, a0_ref, a1_ref, bias_ref, m_ref, acc_ref):
    tk = FOX_TK
    tq = FOX_TQ
    n_q = k_ref.shape[2] // tq
    assert tq == 2 * tk and n_q >= 2 and n_q % 2 == 0
    sub = 2
    s_refs = (s0_ref, s1_ref)
    bm_refs = (bm0_ref, bm1_ref)
    p_refs = (p0_ref, p1_ref)
    a_refs = (a0_ref, a1_ref)

    def issue_scores(par, qi, kb, with_max):
        k0 = pl.multiple_of(kb * tk, tk)
        q0 = pl.multiple_of(qi * tq, tq)
        for hh in range(2):
            s = _dot(k_ref[0, hh, pl.ds(k0, tk), :],
                     qt_ref[0, hh * LANES:(hh + 1) * LANES, pl.ds(q0, tq)])
            s_refs[par][hh] = s
            if with_max:
                bm_refs[par][hh] = jnp.max(s, axis=0, keepdims=True)

    def accumulate(par, qi, kb):
        k0 = pl.multiple_of(kb * tk, tk)
        q0 = pl.multiple_of(qi * tq, tq)
        for hh in range(2):
            pv = _dot(vt_ref[0, hh, :, pl.ds(k0, tk)], p_refs[par][hh])
            acc_ref[hh, :, pl.ds(q0, tq)] = a_refs[par][hh] * acc_ref[hh, :, pl.ds(q0, tq)] + pv

    def softmax(idx, qi, mask_id):
        q0 = pl.multiple_of(qi * tq, tq)
        par = idx % 2
        for hh in range(2):
            s = s_refs[par][hh]
            if mask_id is None:
                m_blk = bm_refs[par][hh]
            else:
                s = s + bias_ref[mask_id]
                m_blk = jnp.max(s, axis=0, keepdims=True)
            m_old = m_ref[hh, :, pl.ds(q0, tq)]
            m_new = jnp.maximum(m_old, m_blk)
            m_ref[hh, :, pl.ds(q0, tq)] = m_new
            a_refs[par][hh] = jnp.exp2(m_old - m_new)
            p_refs[par][hh] = jnp.exp2(s - m_new).astype(BF16)

    def finalize(qi):
        q0 = pl.multiple_of(qi * tq, tq)
        halves = [acc_ref[hh, :HEAD_DIM, pl.ds(q0, tq)]
                  / acc_ref[hh, HEAD_DIM:HEAD_DIM + 1, pl.ds(q0, tq)] for hh in range(2)]
        o_ref[0, pl.ds(q0, tq), :] = jnp.concatenate(halves, axis=0).T.astype(BF16)

    def step(idx, prev, cur, nxt, mask_id, after_accumulate=None):
        accumulate((idx + 1) % 2, *prev)
        if after_accumulate is not None:
            after_accumulate()
        issue_scores((idx + 1) % 2, *nxt, with_max=mask_id is None)
        softmax(idx, cur[0], mask_id)

    key_pos = lax.broadcasted_iota(jnp.int32, (tk, tq), 0)
    qry_pos = lax.broadcasted_iota(jnp.int32, (tk, tq), 1)
    for d in range(sub):
        bias_ref[d] = jnp.where(key_pos + d * tk <= qry_pos, 0.0, -jnp.inf)
    m_ref[...] = jnp.full(m_ref.shape, -jnp.inf, F32)
    acc_ref[...] = jnp.zeros(acc_ref.shape, F32)
    p_refs[1][...] = jnp.zeros(p_refs[1].shape, BF16)
    a_refs[1][...] = jnp.zeros(a_refs[1].shape, F32)

    def below_next(qi, kb):
        more = kb + 1 < qi * sub
        wrap = qi + 1 < n_q
        return (jnp.where(more, qi, jnp.where(wrap, qi + 1, 0)),
                jnp.where(more, kb + 1, 0))

    issue_scores(0, 1, 0, with_max=True)

    def below_body(_, carry):
        prev, cur = carry[:2], carry[2:]
        for idx in range(STEPS_PER_ITER):
            nxt = below_next(*cur)
            step(idx, prev, cur, nxt, None)
            prev, cur = cur, nxt
        return prev + cur

    n_below = sub * n_q * (n_q - 1) // 2
    assert STEPS_PER_ITER == 2 * sub and n_below % STEPS_PER_ITER == 0
    zero = jnp.int32(0)
    carry = lax.fori_loop(0, n_below // STEPS_PER_ITER, below_body,
                          (zero, zero, jnp.int32(1), zero))

    def diag_body(j, prev):
        qa = 2 * j
        qb = qa + 1
        ka = qa * sub
        kb = qb * sub
        step(0, prev, (qa, ka), (qa, ka + 1), 0)
        step(1, (qa, ka), (qa, ka + 1), (qb, kb), 1)
        step(2, (qa, ka + 1), (qb, kb), (qb, kb + 1), 0, after_accumulate=lambda: finalize(qa))
        qn = jnp.minimum(qb + 1, n_q - 1)
        step(3, (qb, kb), (qb, kb + 1), (qn, qn * sub), 1)

        @pl.when(j >= 1)
        def _():
            finalize(qa - 1)

        return (qb, kb + 1)

    prev = lax.fori_loop(0, n_q // 2, diag_body, carry[:2])
    accumulate(1, *prev)
    finalize(n_q - 1)


def _fox_attention(kp, qt, vt):
    b, _, s, _ = kp.shape
    pairs = FOX_HEADS // 2
    return pl.pallas_call(
        _fox_kernel,
        grid=(b, pairs),
        in_specs=[
            pl.BlockSpec((1, 2, s, LANES), lambda i, p: (i, p, 0, 0)),
            pl.BlockSpec((1, 2 * LANES, s), lambda i, p: (i, p, 0)),
            pl.BlockSpec((1, 2, V_ROWS, s), lambda i, p: (i, p, 0, 0)),
        ],
        out_specs=pl.BlockSpec((1, s, LANES), lambda i, p: (i, 0, p)),
        out_shape=jax.ShapeDtypeStruct((b, s, MAIN_WIDTH), BF16),
        scratch_shapes=[
            pltpu.VMEM((2, FOX_TK, FOX_TQ), F32),
            pltpu.VMEM((2, FOX_TK, FOX_TQ), F32),
            pltpu.VMEM((2, 1, FOX_TQ), F32),
            pltpu.VMEM((2, 1, FOX_TQ), F32),
            pltpu.VMEM((2, FOX_TK, FOX_TQ), BF16),
            pltpu.VMEM((2, FOX_TK, FOX_TQ), BF16),
            pltpu.VMEM((2, 1, FOX_TQ), F32),
            pltpu.VMEM((2, 1, FOX_TQ), F32),
            pltpu.VMEM((FOX_TQ // FOX_TK, FOX_TK, FOX_TQ), F32),
            pltpu.VMEM((2, 1, s), F32),
            pltpu.VMEM((2, V_ROWS, s), F32),
        ],
        compiler_params=_params("arbitrary", "arbitrary"),
        name="fox_attention",
    )(kp, qt, vt)


def _mixer_b_kernel(h_ref, attn_ref, qm_ref, kt_ref, vm_ref, w_out_ref, g_post_ref, o_ref):
    h = h_ref[...]
    s_mem = _memory_scores(qm_ref[...], kt_ref[0, 0])
    y_main = _dot(attn_ref[...], w_out_ref[:MAIN_WIDTH, :])
    mem_o = _memory_readout(s_mem, vm_ref[0, 0]).astype(BF16)
    y = y_main + _dot(mem_o, w_out_ref[MAIN_WIDTH:, :])
    o_ref[...] = h + _rms(y, g_post_ref[...])


def _mixer_b(h, attn, qm, kt, vm, layer, w_out, g_post, seq, tm):
    t, d = h.shape
    per_batch = seq // tm
    tile = lambda w: pl.BlockSpec((tm, w), lambda i: (i, 0))
    return pl.pallas_call(
        _mixer_b_kernel,
        grid=(t // tm,),
        in_specs=[tile(d), tile(MAIN_WIDTH), tile(MEM_WIDTH),
                  pl.BlockSpec((1, 1) + kt.shape[2:], lambda i: (layer, i // per_batch, 0, 0)),
                  pl.BlockSpec((1, 1) + vm.shape[2:], lambda i: (layer, i // per_batch, 0, 0)),
                  _full(w_out.shape), _full(g_post.shape)],
        out_specs=tile(d),
        out_shape=jax.ShapeDtypeStruct((t, d), F32),
        compiler_params=_params("arbitrary"),
        name="mixer_b",
    )(h, attn, qm, kt, vm, w_out, g_post)


def kernel(x, mem, ln_mix_pre, ln_mix_post, ln_ffn_pre, ln_ffn_post, ln_mem, w_mem_kv, w_out,
           w_ffn_gate, w_ffn_up, w_ffn_down, w_in_a, w_spatial, b_spatial, ln_v_g, ln_v_b,
           ln_shared, w_shared_kv, b_forget, w_in_b):
    b, s, d = x.shape
    t = b * s
    row = lambda g: g.reshape(1, -1)
    bf = lambda w: w.astype(BF16)

    kt_mem, v_mem = _memkv(mem, ln_mem, w_mem_kv)
    h = x.reshape(t, d)

    ffn_weights = (w_ffn_gate, w_ffn_up, w_ffn_down)
    proj_weights = (
        (w_shared_kv, None, (0, MAIN_WIDTH), False),
        (w_shared_kv, None, (MAIN_WIDTH, MAIN_WIDTH), True),
        (w_in_b, 0, (0, MAIN_WIDTH), True),
        (w_in_b, 0, (MAIN_WIDTH, MEM_WIDTH), False),
    )
    h, wg, wu, wd, w_k, w_vt, w_qt, w_qm = _mixer_a(
        h, kt_mem, v_mem, 0, row(ln_mix_pre[0]), bf(w_in_a[0]), w_spatial[0],
        b_spatial[0].reshape(A_GROUPS, CHUNK, 1), row(ln_v_g[0]), row(ln_v_b[0]),
        bf(w_out[0]), row(ln_mix_post[0]), s, MIXER_A_TILE, ffn_weights + proj_weights, 0)
    h = _ffn(h, row(ln_ffn_pre[0]), wg, wu, wd, row(ln_ffn_post[0]), FFN_TILE)

    pad = LANES - FOX_HEADS
    w_f = jnp.pad(w_shared_kv[:, 2 * MAIN_WIDTH:], ((0, 0), (0, pad)))
    b_f = jnp.pad(b_forget, (0, pad)).reshape(1, LANES)
    kp, qt, vt, qm, wg, wu, wd, w_out1 = _proj_b(
        h.reshape(b, s, d), row(ln_shared), row(ln_mix_pre[1]), w_k, w_vt, bf(w_f), b_f, w_qt, w_qm,
        PROJ_TILE,
        ffn_weights + (w_out,), 1)
    attn = _fox_attention(kp, qt, vt)
    h = _mixer_b(h, attn.reshape(t, MAIN_WIDTH), qm.reshape(t, MEM_WIDTH), kt_mem, v_mem, 1,
                 w_out1, row(ln_mix_post[1]), s, TOKEN_TILE)
    h = _ffn(h, row(ln_ffn_pre[1]), wg, wu, wd, row(ln_ffn_post[1]), FFN_TILE)
    return h.reshape(b, s, d)
```

```python
import functools
import math

import numpy as np
import jax
import jax.numpy as jnp
from jax import lax
from jax.experimental import pallas as pl
from jax.experimental.pallas import tpu as pltpu

F32 = jnp.float32
BF16 = jnp.bfloat16

D_MODEL = 1024
HEAD_DIM = 64
MEM_HEADS = 4
MEM_WIDTH = MEM_HEADS * HEAD_DIM
MAIN_WIDTH = D_MODEL - MEM_WIDTH
CHUNK = 128
A_GROUPS = 6
FOX_HEADS = MAIN_WIDTH // HEAD_DIM
RMS_EPS = 1e-6
LN_EPS = 1e-5
SCALE = HEAD_DIM ** -0.5
LOG2E = math.log2(math.e)
LANES = 128
VMEM_LIMIT = 56 * 1024 * 1024
TOKEN_TILE = 1024
MIXER_A_TILE = 1024
MIXER_SUB_TILES = 4
PROJ_TILE = 1024
PROJ_SUB_TILES = 4
FFN_TILE = 1024
FFN_SUB_ROWS = 256

N_PIECES = 3
PIECE_STRIDE = 16
CT_ROW = 36
V_ROWS = 80
FOX_TK = 256
FOX_TQ = 512
COL_CHUNK = 256
STEPS_PER_ITER = 4

NT_DIMS = (((1,), (1,)), ((), ()))


def _dot(a, b):
    return jnp.dot(a, b, preferred_element_type=F32)


def _dot_nt(a, b):
    return lax.dot_general(a, b, NT_DIMS, preferred_element_type=F32)


def _rms(x, g):
    return (x * lax.rsqrt(jnp.mean(x * x, axis=-1, keepdims=True) + RMS_EPS)) * g


def _split_bf16(x):
    p1 = x.astype(BF16)
    r1 = x - p1.astype(F32)
    p2 = r1.astype(BF16)
    p3 = (r1 - p2.astype(F32)).astype(BF16)
    return p1, p2, p3


def _params(*sem):
    return pltpu.CompilerParams(dimension_semantics=sem, vmem_limit_bytes=VMEM_LIMIT)


def _full(shape):
    n = len(shape)
    return pl.BlockSpec(shape, lambda *_: (0,) * n)


def _cast_specs(params, layer, n_steps, flat_step):
    in_specs, out_specs, out_shapes = [], [], []
    for w in params:
        _, r, c = w.shape
        rows = r // n_steps
        assert rows * n_steps == r and rows % 16 == 0
        in_specs.append(pl.BlockSpec((1, rows, c), lambda *g: (layer, flat_step(*g), 0)))
        out_specs.append(pl.BlockSpec((rows, c), lambda *g: (flat_step(*g), 0)))
        out_shapes.append(jax.ShapeDtypeStruct((r, c), BF16))
    return in_specs, out_specs, out_shapes


def _cast_slabs(src_refs, dst_refs):
    for src, dst in zip(src_refs, dst_refs):
        dst[...] = src[0].astype(BF16)


def _memkv_kernel(mem_ref, g_ref, w_ref, kt_ref, v_ref):
    m = mem_ref.shape[1]
    n = _rms(mem_ref[0], g_ref[0])
    kv = _dot(n.astype(BF16), w_ref[0].astype(BF16))
    k_t = kv[:, :MEM_WIDTH].T
    v = kv[:, MEM_WIDTH:]
    head_of_row = lax.broadcasted_iota(jnp.int32, (MEM_WIDTH, 1), 0) // HEAD_DIM
    head_of_lane = lax.broadcasted_iota(jnp.int32, (1, MEM_WIDTH), 1) // HEAD_DIM
    for hh in range(MEM_HEADS):
        kt_ref[0, 0, :, hh * m:(hh + 1) * m] = jnp.where(head_of_row == hh, k_t, 0.0).astype(BF16)
        v_ref[0, 0, hh * m:(hh + 1) * m, :] = jnp.where(head_of_lane == hh, v, 0.0).astype(BF16)


def _memkv(mem, ln_mem, w_mem_kv):
    depth = ln_mem.shape[0]
    b, m, d = mem.shape
    n_out = w_mem_kv.shape[-1]
    return pl.pallas_call(
        _memkv_kernel,
        grid=(depth, b),
        in_specs=[
            pl.BlockSpec((1, m, d), lambda l, i: (i, 0, 0)),
            pl.BlockSpec((1, 1, d), lambda l, i: (l, 0, 0)),
            pl.BlockSpec((1, d, n_out), lambda l, i: (l, 0, 0)),
        ],
        out_specs=[
            pl.BlockSpec((1, 1, MEM_WIDTH, MEM_HEADS * m), lambda l, i: (l, i, 0, 0)),
            pl.BlockSpec((1, 1, MEM_HEADS * m, MEM_WIDTH), lambda l, i: (l, i, 0, 0)),
        ],
        out_shape=[
            jax.ShapeDtypeStruct((depth, b, MEM_WIDTH, MEM_HEADS * m), BF16),
            jax.ShapeDtypeStruct((depth, b, MEM_HEADS * m, MEM_WIDTH), BF16),
        ],
        compiler_params=_params("arbitrary", "arbitrary"),
        name="memkv",
    )(mem, ln_mem.reshape(depth, 1, d), w_mem_kv)


def _memory_scores(qm, kt):
    return _dot(qm, kt)


def _memory_readout(s_all, v):
    m = s_all.shape[1] // MEM_HEADS
    probs = []
    for hh in range(MEM_HEADS):
        s = s_all[:, hh * m:(hh + 1) * m]
        p = jnp.exp(s - jnp.max(s, axis=-1, keepdims=True))
        inv = 1.0 / jnp.sum(p, axis=-1, keepdims=True)
        probs.append((p * inv).astype(BF16))
    return _dot(jnp.concatenate(probs, axis=1), v)


def _mixer_a_kernel(n_cast, h_ref, kt_ref, vm_ref, g_pre_ref, w_in_ref, wsp_ref, bsp_ref, lng_ref,
                    lnb_ref, w_out_ref, g_post_ref, *rest):
    cast_src, (o_ref, *cast_dst), mixed_ref = rest[:n_cast], rest[n_cast:-1], rest[-1]
    _cast_slabs(cast_src, cast_dst)
    sub = h_ref.shape[0] // MIXER_SUB_TILES
    n_chunks = sub // CHUNK
    row = lax.broadcasted_iota(jnp.int32, (CHUNK, CHUNK), 0)
    col = lax.broadcasted_iota(jnp.int32, (CHUNK, CHUNK), 1)
    causal = row >= col
    w_sp = [jnp.where(causal, wsp_ref[g], 0.0).astype(BF16) for g in range(A_GROUPS)]

    def in_proj(r):
        h = h_ref[r * sub:(r + 1) * sub, :]
        a = _rms(h, g_pre_ref[...]).astype(BF16)
        return h, _dot(a, w_in_ref[...])

    def mix(r, h, proj):
        base = r * sub
        s_mem = _memory_scores((proj[:, 2 * MAIN_WIDTH:] * SCALE).astype(BF16), kt_ref[0, 0])
        u = jax.nn.gelu(proj[:, :MAIN_WIDTH])
        v = jax.nn.gelu(proj[:, MAIN_WIDTH:2 * MAIN_WIDTH])
        mu = jnp.mean(v, axis=-1, keepdims=True)
        vc = v - mu
        vn = vc * lax.rsqrt(jnp.mean(vc * vc, axis=-1, keepdims=True) + LN_EPS)
        vn = (vn * lng_ref[...] + lnb_ref[...]).astype(BF16)
        for g in range(A_GROUPS):
            bias = bsp_ref[g]
            cols = slice(g * CHUNK, (g + 1) * CHUNK)
            v_g = jnp.concatenate([vn[c * CHUNK:(c + 1) * CHUNK, cols] for c in range(n_chunks)],
                                  axis=1)
            s_g = _dot(w_sp[g], v_g)
            for c in range(n_chunks):
                rows = slice(c * CHUNK, (c + 1) * CHUNK)
                s = s_g[:, c * CHUNK:(c + 1) * CHUNK] + bias
                mixed_ref[base + c * CHUNK:base + (c + 1) * CHUNK, cols] = (
                    u[rows, cols] * s).astype(BF16)
        mixed_ref[base:base + sub, MAIN_WIDTH:] = _memory_readout(s_mem, vm_ref[0, 0]).astype(BF16)
        y = _dot(mixed_ref[base:base + sub, :], w_out_ref[...])
        o_ref[base:base + sub, :] = h + _rms(y, g_post_ref[...])

    nxt = in_proj(0)
    for r in range(MIXER_SUB_TILES):
        cur = nxt
        if r + 1 < MIXER_SUB_TILES:
            nxt = in_proj(r + 1)
        mix(r, *cur)


def _mixer_a(h, kt, vm, layer, g_pre, w_in, wsp, bsp, lng, lnb, w_out, g_post, seq, tm,
             cast_params, cast_layer):
    t, d = h.shape
    per_batch = seq // tm
    c_in, c_out, c_shape = _cast_specs(cast_params, cast_layer, t // tm, lambda i: i)
    return pl.pallas_call(
        functools.partial(_mixer_a_kernel, len(cast_params)),
        grid=(t // tm,),
        in_specs=[
            pl.BlockSpec((tm, d), lambda i: (i, 0)),
            pl.BlockSpec((1, 1) + kt.shape[2:], lambda i: (layer, i // per_batch, 0, 0)),
            pl.BlockSpec((1, 1) + vm.shape[2:], lambda i: (layer, i // per_batch, 0, 0)),
            _full(g_pre.shape), _full(w_in.shape), _full(wsp.shape), _full(bsp.shape),
            _full(lng.shape), _full(lnb.shape), _full(w_out.shape), _full(g_post.shape),
        ] + c_in,
        out_specs=[pl.BlockSpec((tm, d), lambda i: (i, 0))] + c_out,
        out_shape=[jax.ShapeDtypeStruct((t, d), F32)] + c_shape,
        scratch_shapes=[pltpu.VMEM((tm, d), BF16)],
        compiler_params=_params("arbitrary"),
        name="mixer_a",
    )(h, kt, vm, g_pre, w_in, wsp, bsp, lng, lnb, w_out, g_post, *cast_params)


def _ffn_kernel(h_ref, g_pre_ref, wg_ref, wu_ref, wd_ref, g_post_ref, o_ref):
    sub = FFN_SUB_ROWS
    for r in range(h_ref.shape[0] // sub):
        rows = slice(r * sub, (r + 1) * sub)
        h = h_ref[rows, :]
        f = _rms(h, g_pre_ref[...]).astype(BF16)
        gate = _dot(f, wg_ref[...])
        up = _dot(f, wu_ref[...])
        act = (jax.nn.silu(gate) * up).astype(BF16)
        y = _dot(act, wd_ref[...])
        o_ref[rows, :] = h + _rms(y, g_post_ref[...])


def _ffn(h, g_pre, wg, wu, wd, g_post, tm):
    t, d = h.shape
    return pl.pallas_call(
        _ffn_kernel,
        grid=(t // tm,),
        in_specs=[
            pl.BlockSpec((tm, d), lambda i: (i, 0)),
            _full(g_pre.shape), _full(wg.shape), _full(wu.shape), _full(wd.shape),
            _full(g_post.shape),
        ],
        out_specs=pl.BlockSpec((tm, d), lambda i: (i, 0)),
        out_shape=jax.ShapeDtypeStruct((t, d), F32),
        compiler_params=_params("arbitrary"),
        name="ffn",
    )(h, g_pre, wg, wu, wd, g_post)


def _gate_selector():
    g = np.zeros((LANES, LANES), np.float32)
    for i in range(N_PIECES):
        for hd in range(FOX_HEADS):
            g[PIECE_STRIDE * i + hd, HEAD_DIM + N_PIECES * hd + i] = -1.0
    return jnp.asarray(g, dtype=BF16)


def _pack_pieces(pieces):
    packed = pieces[0].astype(F32)
    for i in range(1, N_PIECES):
        packed = packed + pltpu.roll(pieces[i].astype(F32), PIECE_STRIDE * i, axis=1)
    return packed.astype(BF16)


def _proj_b_kernel(n_cast, h_ref, g_sh_ref, g_pre_ref, w_k_ref, w_vt_ref, w_f_ref, b_f_ref,
                   w_qt_ref, w_qm_ref, gsel_ref, *rest):
    cast_src, carry_ref = rest[:n_cast], rest[-1]
    kp_ref, qt_ref, vt_ref, qm_ref, *cast_dst = rest[n_cast:-1]
    _cast_slabs(cast_src, cast_dst)
    sub = h_ref.shape[1] // PROJ_SUB_TILES

    @pl.when(pl.program_id(1) == 0)
    def _():
        carry_ref[...] = jnp.zeros_like(carry_ref)

    lane = lax.broadcasted_iota(jnp.int32, (1, LANES), 1)
    row = lax.broadcasted_iota(jnp.int32, (sub, sub), 0)
    col = lax.broadcasted_iota(jnp.int32, (sub, sub), 1)
    tri = (row >= col).astype(BF16)
    low = lane < HEAD_DIM
    ones_lanes = jnp.where((lane >= HEAD_DIM + CT_ROW) & (lane < HEAD_DIM + CT_ROW + N_PIECES),
                           1.0, 0.0)
    rid = lax.broadcasted_iota(jnp.int32, (HEAD_DIM, 1), 0)
    rid_v = lax.broadcasted_iota(jnp.int32, (V_ROWS - HEAD_DIM, sub), 0)
    ones_row = jnp.where(rid_v == 0, 1.0, 0.0).astype(BF16)

    def project(r):
        rows = slice(r * sub, (r + 1) * sub)
        h = h_ref[0, rows, :]
        hn = h * lax.rsqrt(jnp.mean(h * h, axis=-1, keepdims=True) + RMS_EPS)
        s_in = (hn * g_sh_ref[...]).astype(BF16)
        a = (hn * g_pre_ref[...]).astype(BF16)
        log_f = jax.nn.log_sigmoid(_dot(s_in, w_f_ref[...]) + b_f_ref[...])
        log_f = jnp.where(lane < FOX_HEADS, log_f, 0.0)
        k = _dot(s_in, w_k_ref[...])
        qt = _dot_nt(w_qt_ref[...], a) * (SCALE * LOG2E)
        vt = _dot_nt(w_vt_ref[...], s_in)
        qm_ref[0, rows, :] = (_dot(a, w_qm_ref[...]) * SCALE).astype(BF16)
        return log_f, k, qt, vt

    def assemble(r, log_f, k, qt, vt):
        rows = slice(r * sub, (r + 1) * sub)
        sums = _dot(tri, _pack_pieces(_split_bf16(log_f)))
        c = sums + carry_ref[...]
        for i in range(1, N_PIECES):
            c = c + pltpu.roll(sums, LANES - PIECE_STRIDE * i, axis=1)
        c = jnp.where(lane < FOX_HEADS, c, 0.0)
        carry_ref[...] = c[sub - 1:sub, :]
        c2 = c * LOG2E

        aug = _dot(_pack_pieces(_split_bf16(c2)), gsel_ref[...]) + ones_lanes
        for p in range(FOX_HEADS // 2):
            blk = k[:, p * LANES:(p + 1) * LANES]
            kp_ref[0, 2 * p, rows, :] = jnp.where(low, blk, aug).astype(BF16)
            kp_ref[0, 2 * p + 1, rows, :] = jnp.where(
                low, pltpu.roll(blk, HEAD_DIM, axis=1), aug).astype(BF16)

        t1, t2, t3 = _split_bf16(c2.T)
        for hd in range(FOX_HEADS):
            sel = jnp.where((rid >= N_PIECES * hd) & (rid < N_PIECES * (hd + 1)), 1.0, 0.0)
            aug_q = jnp.where(rid == CT_ROW, t1[hd:hd + 1, :].astype(F32),
                              jnp.where(rid == CT_ROW + 1, t2[hd:hd + 1, :].astype(F32),
                                        jnp.where(rid == CT_ROW + 2, t3[hd:hd + 1, :].astype(F32),
                                                  sel)))
            qt_ref[0, hd * LANES:hd * LANES + HEAD_DIM, rows] = (
                qt[hd * HEAD_DIM:(hd + 1) * HEAD_DIM, :].astype(BF16))
            qt_ref[0, hd * LANES + HEAD_DIM:(hd + 1) * LANES, rows] = aug_q.astype(BF16)
            vt_ref[0, hd, :HEAD_DIM, rows] = vt[hd * HEAD_DIM:(hd + 1) * HEAD_DIM, :].astype(BF16)
            vt_ref[0, hd, HEAD_DIM:, rows] = ones_row

    nxt = project(0)
    for r in range(PROJ_SUB_TILES):
        cur = nxt
        if r + 1 < PROJ_SUB_TILES:
            nxt = project(r + 1)
        assemble(r, *cur)


def _proj_b(h3, g_sh, g_pre, w_k, w_vt, w_f, b_f, w_qt, w_qm, tm, cast_params, cast_layer):
    b, s, d = h3.shape
    gsel = _gate_selector()
    per_batch = s // tm
    c_in, c_out, c_shape = _cast_specs(cast_params, cast_layer, b * per_batch,
                                       lambda i, j: i * per_batch + j)
    return pl.pallas_call(
        functools.partial(_proj_b_kernel, len(cast_params)),
        grid=(b, per_batch),
        in_specs=[pl.BlockSpec((1, tm, d), lambda i, j: (i, j, 0)),
                  _full(g_sh.shape), _full(g_pre.shape), _full(w_k.shape), _full(w_vt.shape),
                  _full(w_f.shape), _full(b_f.shape), _full(w_qt.shape), _full(w_qm.shape),
                  _full(gsel.shape)] + c_in,
        out_specs=[
            pl.BlockSpec((1, FOX_HEADS, tm, LANES), lambda i, j: (i, 0, j, 0)),
            pl.BlockSpec((1, FOX_HEADS * LANES, tm), lambda i, j: (i, 0, j)),
            pl.BlockSpec((1, FOX_HEADS, V_ROWS, tm), lambda i, j: (i, 0, 0, j)),
            pl.BlockSpec((1, tm, MEM_WIDTH), lambda i, j: (i, j, 0)),
        ] + c_out,
        out_shape=[
            jax.ShapeDtypeStruct((b, FOX_HEADS, s, LANES), BF16),
            jax.ShapeDtypeStruct((b, FOX_HEADS * LANES, s), BF16),
            jax.ShapeDtypeStruct((b, FOX_HEADS, V_ROWS, s), BF16),
            jax.ShapeDtypeStruct((b, s, MEM_WIDTH), BF16),
        ] + c_shape,
        scratch_shapes=[pltpu.VMEM((1, LANES), F32)],
        compiler_params=_params("arbitrary", "arbitrary"),
        name="proj_b",
    )(h3, g_sh, g_pre, w_k, w_vt, w_f, b_f, w_qt, w_qm, gsel, *cast_params)


def _fox_kernel(k_ref, qt_ref, vt_ref, o_ref, s0_ref, s1_ref, bm0_ref, bm1_ref,
                p0_ref, p1_ref, a0_ref, a1_ref, bias_ref, m_ref, acc_ref):
    tk = FOX_TK
    tq = FOX_TQ
    n_q = k_ref.shape[2] // tq
    assert tq == 2 * tk and n_q >= 2 and n_q % 2 == 0
    sub = 2
    s_refs = (s0_ref, s1_ref)
    bm_refs = (bm0_ref, bm1_ref)
    p_refs = (p0_ref, p1_ref)
    a_refs = (a0_ref, a1_ref)

    def issue_scores(par, qi, kb, with_max):
        k0 = pl.multiple_of(kb * tk, tk)
        q0 = pl.multiple_of(qi * tq, tq)
        for hh in range(2):
            for c0 in range(0, tq, COL_CHUNK):
                s = _dot(k_ref[0, hh, pl.ds(k0, tk), :],
                         qt_ref[0, hh * LANES:(hh + 1) * LANES, pl.ds(q0 + c0, COL_CHUNK)])
                s_refs[par][hh, :, c0:c0 + COL_CHUNK] = s
                if with_max:
                    bm_refs[par][hh, :, c0:c0 + COL_CHUNK] = jnp.max(s, axis=0, keepdims=True)

    def accumulate(par, qi, kb):
        k0 = pl.multiple_of(kb * tk, tk)
        q0 = pl.multiple_of(qi * tq, tq)
        for hh in range(2):
            pv = _dot(vt_ref[0, hh, :, pl.ds(k0, tk)], p_refs[par][hh])
            acc_ref[hh, :, pl.ds(q0, tq)] = a_refs[par][hh] * acc_ref[hh, :, pl.ds(q0, tq)] + pv

    def softmax(idx, qi, mask_id):
        q0 = pl.multiple_of(qi * tq, tq)
        par = idx % 2
        for hh in range(2):
            for c0 in range(0, tq, COL_CHUNK):
                cols = slice(c0, c0 + COL_CHUNK)
                s = s_refs[par][hh, :, cols]
                if mask_id is None:
                    m_blk = bm_refs[par][hh, :, cols]
                else:
                    s = s + bias_ref[mask_id, :, cols]
                    m_blk = jnp.max(s, axis=0, keepdims=True)
                m_old = m_ref[hh, :, pl.ds(q0 + c0, COL_CHUNK)]
                m_new = jnp.maximum(m_old, m_blk)
                m_ref[hh, :, pl.ds(q0 + c0, COL_CHUNK)] = m_new
                a_refs[par][hh, :, cols] = jnp.exp2(m_old - m_new)
                p_refs[par][hh, :, cols] = jnp.exp2(s - m_new).astype(BF16)

    def finalize(qi):
        q0 = pl.multiple_of(qi * tq, tq)
        halves = [acc_ref[hh, :HEAD_DIM, pl.ds(q0, tq)]
                  / acc_ref[hh, HEAD_DIM:HEAD_DIM + 1, pl.ds(q0, tq)] for hh in range(2)]
        o_ref[0, pl.ds(q0, tq), :] = jnp.concatenate(halves, axis=0).T.astype(BF16)

    def step(idx, prev, cur, nxt, mask_id, after_accumulate=None):
        accumulate((idx + 1) % 2, *prev)
        if after_accumulate is not None:
            after_accumulate()
        issue_scores((idx + 1) % 2, *nxt, with_max=mask_id is None)
        softmax(idx, cur[0], mask_id)

    key_pos = lax.broadcasted_iota(jnp.int32, (tk, tq), 0)
    qry_pos = lax.broadcasted_iota(jnp.int32, (tk, tq), 1)
    for d in range(sub):
        bias_ref[d] = jnp.where(key_pos + d * tk <= qry_pos, 0.0, -jnp.inf)
    m_ref[...] = jnp.full(m_ref.shape, -jnp.inf, F32)
    acc_ref[...] = jnp.zeros(acc_ref.shape, F32)
    p_refs[1][...] = jnp.zeros(p_refs[1].shape, BF16)
    a_refs[1][...] = jnp.zeros(a_refs[1].shape, F32)

    def below_next(qi, kb):
        more = kb + 1 < qi * sub
        wrap = qi + 1 < n_q
        return (jnp.where(more, qi, jnp.where(wrap, qi + 1, 0)),
                jnp.where(more, kb + 1, 0))

    issue_scores(0, 1, 0, with_max=True)

    def below_body(_, carry):
        prev, cur = carry[:2], carry[2:]
        for idx in range(STEPS_PER_ITER):
            nxt = below_next(*cur)
            step(idx, prev, cur, nxt, None)
            prev, cur = cur, nxt
        return prev + cur

    n_below = sub * n_q * (n_q - 1) // 2
    assert STEPS_PER_ITER == 2 * sub and n_below % STEPS_PER_ITER == 0
    zero = jnp.int32(0)
    carry = lax.fori_loop(0, n_below // STEPS_PER_ITER, below_body,
                          (zero, zero, jnp.int32(1), zero))

    def diag_body(j, prev):
        qa = 2 * j
        qb = qa + 1
        ka = qa * sub
        kb = qb * sub
        step(0, prev, (qa, ka), (qa, ka + 1), 0)
        step(1, (qa, ka), (qa, ka + 1), (qb, kb), 1)
        step(2, (qa, ka + 1), (qb, kb), (qb, kb + 1), 0, after_accumulate=lambda: finalize(qa))
        qn = jnp.minimum(qb + 1, n_q - 1)
        step(3, (qb, kb), (qb, kb + 1), (qn, qn * sub), 1)

        @pl.when(j >= 1)
        def _():
            finalize(qa - 1)

        return (qb, kb + 1)

    prev = lax.fori_loop(0, n_q // 2, diag_body, carry[:2])
    accumulate(1, *prev)
    finalize(n_q - 1)


def _fox_attention(kp, qt, vt):
    b, _, s, _ = kp.shape
    pairs = FOX_HEADS // 2
    return pl.pallas_call(
        _fox_kernel,
        grid=(b, pairs),
        in_specs=[
            pl.BlockSpec((1, 2, s, LANES), lambda i, p: (i, p, 0, 0)),
            pl.BlockSpec((1, 2 * LANES, s), lambda i, p: (i, p, 0)),
            pl.BlockSpec((1, 2, V_ROWS, s), lambda i, p: (i, p, 0, 0)),
        ],
        out_specs=pl.BlockSpec((1, s, LANES), lambda i, p: (i, 0, p)),
        out_shape=jax.ShapeDtypeStruct((b, s, MAIN_WIDTH), BF16),
        scratch_shapes=[
            pltpu.VMEM((2, FOX_TK, FOX_TQ), F32),
            pltpu.VMEM((2, FOX_TK, FOX_TQ), F32),
            pltpu.VMEM((2, 1, FOX_TQ), F32),
            pltpu.VMEM((2, 1, FOX_TQ), F32),
            pltpu.VMEM((2, FOX_TK, FOX_TQ), BF16),
            pltpu.VMEM((2, FOX_TK, FOX_TQ), BF16),
            pltpu.VMEM((2, 1, FOX_TQ), F32),
            pltpu.VMEM((2, 1, FOX_TQ), F32),
            pltpu.VMEM((FOX_TQ // FOX_TK, FOX_TK, FOX_TQ), F32),
            pltpu.VMEM((2, 1, s), F32),
            pltpu.VMEM((2, V_ROWS, s), F32),
        ],
        compiler_params=_params("arbitrary", "arbitrary"),
        name="fox_attention",
    )(kp, qt, vt)


def _mixer_b_kernel(h_ref, attn_ref, qm_ref, kt_ref, vm_ref, w_out_ref, g_post_ref, o_ref):
    h = h_ref[...]
    s_mem = _memory_scores(qm_ref[...], kt_ref[0, 0])
    y_main = _dot(attn_ref[...], w_out_ref[:MAIN_WIDTH, :])
    mem_o = _memory_readout(s_mem, vm_ref[0, 0]).astype(BF16)
    y = y_main + _dot(mem_o, w_out_ref[MAIN_WIDTH:, :])
    o_ref[...] = h + _rms(y, g_post_ref[...])


def _mixer_b(h, attn, qm, kt, vm, layer, w_out, g_post, seq, tm):
    t, d = h.shape
    per_batch = seq // tm
    tile = lambda w: pl.BlockSpec((tm, w), lambda i: (i, 0))
    return pl.pallas_call(
        _mixer_b_kernel,
        grid=(t // tm,),
        in_specs=[tile(d), tile(MAIN_WIDTH), tile(MEM_WIDTH),
                  pl.BlockSpec((1, 1) + kt.shape[2:], lambda i: (layer, i // per_batch, 0, 0)),
                  pl.BlockSpec((1, 1) + vm.shape[2:], lambda i: (layer, i // per_batch, 0, 0)),
                  _full(w_out.shape), _full(g_post.shape)],
        out_specs=tile(d),
        out_shape=jax.ShapeDtypeStruct((t, d), F32),
        compiler_params=_params("arbitrary"),
        name="mixer_b",
    )(h, attn, qm, kt, vm, w_out, g_post)


def kernel(x, mem, ln_mix_pre, ln_mix_post, ln_ffn_pre, ln_ffn_post, ln_mem, w_mem_kv, w_out,
           w_ffn_gate, w_ffn_up, w_ffn_down, w_in_a, w_spatial, b_spatial, ln_v_g, ln_v_b,
           ln_shared, w_shared_kv, b_forget, w_in_b):
    b, s, d = x.shape
    t = b * s
    row = lambda g: g.reshape(1, -1)
    bf = lambda w: w.astype(BF16)

    kt_mem, v_mem = _memkv(mem, ln_mem, w_mem_kv)
    h = x.reshape(t, d)

    ffn_weights = (w_ffn_gate, w_ffn_up, w_ffn_down)
    h, wg, wu, wd = _mixer_a(
        h, kt_mem, v_mem, 0, row(ln_mix_pre[0]), bf(w_in_a[0]), w_spatial[0],
        b_spatial[0].reshape(A_GROUPS, CHUNK, 1), row(ln_v_g[0]), row(ln_v_b[0]),
        bf(w_out[0]), row(ln_mix_post[0]), s, MIXER_A_TILE, ffn_weights, 0)
    h = _ffn(h, row(ln_ffn_pre[0]), wg, wu, wd, row(ln_ffn_post[0]), FFN_TILE)

    pad = LANES - FOX_HEADS
    w_f = jnp.pad(w_shared_kv[:, 2 * MAIN_WIDTH:], ((0, 0), (0, pad)))
    b_f = jnp.pad(b_forget, (0, pad)).reshape(1, LANES)
    kp, qt, vt, qm, wg, wu, wd, w_out1 = _proj_b(
        h.reshape(b, s, d), row(ln_shared), row(ln_mix_pre[1]),
        bf(w_shared_kv[:, :MAIN_WIDTH]), bf(w_shared_kv[:, MAIN_WIDTH:2 * MAIN_WIDTH].T),
        bf(w_f), b_f, bf(w_in_b[0][:, :MAIN_WIDTH].T), bf(w_in_b[0][:, MAIN_WIDTH:]), PROJ_TILE,
        ffn_weights + (w_out,), 1)
    attn = _fox_attention(kp, qt, vt)
    h = _mixer_b(h, attn.reshape(t, MAIN_WIDTH), qm.reshape(t, MEM_WIDTH), kt_mem, v_mem, 1,
                 w_out1, row(ln_mix_post[1]), s, TOKEN_TILE)
    h = _ffn(h, row(ln_ffn_pre[1]), wg, wu, wd, row(ln_ffn_post[1]), FFN_TILE)
    return h.reshape(b, s, d)
```
